```python
import jax, jax.numpy as jnp
from jax import lax
import numpy as np

D_MODEL = 2048
BATCH = 16
SEQ = 2048
DEPTH = 2

EPS = 1e-5
ROPE_THETA = 500000.0
BLOCK = 128
NEG_INF = -1e30

SWA_WINDOW = 128
A_Q_HEADS = 16
A_KV_HEADS = 4
A_HEAD_DIM = 64
A_Q_COLS = A_Q_HEADS * A_HEAD_DIM
A_KV_COLS = A_KV_HEADS * A_HEAD_DIM
B_WIDTH = 1024
CONV_WIDTH = 3
IN0_SPLITS = [A_Q_COLS, A_Q_COLS + A_KV_COLS, A_Q_COLS + 2 * A_KV_COLS,
              A_Q_COLS + 2 * A_KV_COLS + B_WIDTH, A_Q_COLS + 2 * A_KV_COLS + 2 * B_WIDTH]
IN0_COLS = A_Q_COLS + 2 * A_KV_COLS + 3 * B_WIDTH
MIX0_WIDTH = A_Q_COLS + B_WIDTH

C_Q_RANK = 512
C_HEADS = 16
C_HEAD_DIM = 128
IDX_HEADS = 16
IDX_HEAD_DIM = 64
INDEX_TOPK = 256
IN1_SPLITS = [C_Q_RANK, C_Q_RANK + C_HEAD_DIM, C_Q_RANK + 2 * C_HEAD_DIM,
              C_Q_RANK + 2 * C_HEAD_DIM + IDX_HEAD_DIM]
IN1_COLS = C_Q_RANK + 2 * C_HEAD_DIM + IDX_HEAD_DIM + IDX_HEADS
UQ_COLS = C_HEADS * C_HEAD_DIM + IDX_HEADS * IDX_HEAD_DIM
MIX1_WIDTH = C_HEADS * C_HEAD_DIM

PEER_HEADS = 8
PEER_N_KEYS = 128
PEER_EXPERTS = PEER_N_KEYS * PEER_N_KEYS
PEER_TOPK = 16
PEER_QDIM = 256
PEER_HALF = PEER_QDIM // 2

kernel_name = 'hybrid_swa_conv_dsa_peer'


def rms_norm(x, g):
    xf = x.astype(jnp.float32)
    y = xf * lax.rsqrt(jnp.mean(xf * xf, axis=-1, keepdims=True) + EPS)
    return (y * g.astype(jnp.float32)).astype(x.dtype)


def partial_rope(x, positions):
    dh = x.shape[-1]
    rot = dh // 4
    half = rot // 2
    inv = ROPE_THETA ** (-jnp.arange(half, dtype=jnp.float32) * 2.0 / rot)
    ang = positions.astype(jnp.float32)[:, :, None, None] * inv
    cos, sin = jnp.cos(ang), jnp.sin(ang)
    xf = x.astype(jnp.float32)
    x1, x2, rest = xf[..., :half], xf[..., half:rot], xf[..., rot:]
    out = jnp.concatenate([x1 * cos - x2 * sin, x2 * cos + x1 * sin, rest], axis=-1)
    return out.astype(x.dtype)


def sliding_window_gqa(q, k, v, sinks):
    Bn, S, Hq, dh = q.shape
    Hkv = k.shape[2]
    G = Hq // Hkv
    nb = S // BLOCK
    qb = q.reshape(Bn, nb, BLOCK, Hkv, G, dh)

    def band(t):
        tb = jnp.pad(t, ((0, 0), (BLOCK, 0), (0, 0), (0, 0))).reshape(Bn, nb + 1, BLOCK, Hkv, dh)
        return jnp.concatenate([tb[:, :-1], tb[:, 1:]], axis=2)

    kb, vb = band(k), band(v)
    s = jnp.einsum('bnqhgd,bnkhd->bnhgqk', qb, kb).astype(jnp.float32) * (dh ** -0.5)
    qi = jnp.arange(BLOCK)[:, None]
    kj = jnp.arange(2 * BLOCK)[None, :]
    diff = qi + BLOCK - kj
    in_win = (diff >= 0) & (diff < SWA_WINDOW)
    kpos = jnp.arange(nb)[:, None, None] * BLOCK - BLOCK + kj[None]
    valid = in_win[None] & (kpos >= 0)
    s = jnp.where(valid[None, :, None, None], s, NEG_INF)
    sink = jnp.broadcast_to(
        sinks.astype(jnp.float32).reshape(Hkv, G)[None, None, :, :, None, None],
        s.shape[:-1] + (1,))
    p = jax.nn.softmax(jnp.concatenate([s, sink], axis=-1), axis=-1)[..., :-1]
    o = jnp.einsum('bnhgqk,bnkhd->bnqhgd', p.astype(v.dtype), vb)
    return o.reshape(Bn, S, Hq * dh)


def short_gated_conv(b_gate, c_gate, h, conv_w):
    z = c_gate * h
    W = z.shape[-1]
    y = lax.conv_general_dilated(
        z, conv_w[:, None, :].astype(z.dtype), window_strides=(1,),
        padding=((CONV_WIDTH - 1, 0),), dimension_numbers=('NWC', 'WIO', 'NWC'),
        feature_group_count=W)
    return b_gate * y


def even_mixer(x, positions, norm_g, w_in, sinks, conv_w, w_out):
    Bn, S, _ = x.shape
    h = rms_norm(x, norm_g)
    q, k, v, bg, cg, hx = jnp.split(h @ w_in, IN0_SPLITS, axis=-1)
    q = partial_rope(q.reshape(Bn, S, A_Q_HEADS, A_HEAD_DIM), positions)
    k = partial_rope(k.reshape(Bn, S, A_KV_HEADS, A_HEAD_DIM), positions)
    v = v.reshape(Bn, S, A_KV_HEADS, A_HEAD_DIM)
    a_out = sliding_window_gqa(q, k, v, sinks)
    b_out = short_gated_conv(bg, cg, hx, conv_w)
    return x + jnp.concatenate([a_out, b_out], axis=-1) @ w_out


def indexed_sparse_attention(q, k, v, iq, ik, iw):
    Bn, S, H, dh = q.shape
    nb = S // BLOCK
    k_sel = min(INDEX_TOPK, S // 4)
    key_pos = jnp.arange(S)

    def blockify(t):
        return jnp.moveaxis(t.reshape((Bn, nb, BLOCK) + t.shape[2:]), 1, 0)

    def one_block(args):
        qb, iqb, iwb, qpos = args
        logits = jnp.einsum('bqhd,bsd->bqhs', iqb, ik).astype(jnp.float32)
        score = jnp.einsum('bqh,bqhs->bqs', iwb.astype(jnp.float32), jax.nn.relu(logits))
        causal = key_pos[None, :] <= qpos[:, None]
        score = jnp.where(causal[None], score, NEG_INF)
        _, idx = lax.top_k(score, k_sel)
        kg = jax.vmap(lambda kk, ii: kk[ii])(k, idx)
        vg = jax.vmap(lambda vv, ii: vv[ii])(v, idx)
        sel_valid = idx <= qpos[None, :, None]
        s = jnp.einsum('bqhd,bqkd->bqhk', qb, kg).astype(jnp.float32) * (dh ** -0.5)
        s = jnp.where(sel_valid[:, :, None, :], s, NEG_INF)
        p = jax.nn.softmax(s, axis=-1)
        return jnp.einsum('bqhk,bqkd->bqhd', p.astype(vg.dtype), vg)

    qpos_all = jnp.arange(S).reshape(nb, BLOCK)
    out = lax.map(one_block, (blockify(q), blockify(iq), blockify(iw), qpos_all))
    return jnp.moveaxis(out, 0, 1).reshape(Bn, S, H * dh)


def odd_mixer(x, positions, norm_g, w_in, g_qa, w_uq, w_out):
    Bn, S, _ = x.shape
    h = rms_norm(x, norm_g)
    cq, k, v, ik, iw = jnp.split(h @ w_in, IN1_SPLITS, axis=-1)
    uq = rms_norm(cq, g_qa) @ w_uq
    q, iq = jnp.split(uq, [C_HEADS * C_HEAD_DIM], axis=-1)
    q = partial_rope(q.reshape(Bn, S, C_HEADS, C_HEAD_DIM), positions)
    k = partial_rope(k.reshape(Bn, S, 1, C_HEAD_DIM), positions)[:, :, 0]
    iq = partial_rope(iq.reshape(Bn, S, IDX_HEADS, IDX_HEAD_DIM), positions)
    ik = partial_rope(ik.reshape(Bn, S, 1, IDX_HEAD_DIM), positions)[:, :, 0]
    iw = iw * ((IDX_HEADS ** -0.5) * (IDX_HEAD_DIM ** -0.5))
    o = indexed_sparse_attention(q, k, v, iq, ik, iw)
    return x + o @ w_out


def peer_ffn(h, w_q, sub_keys, u, v):
    Bn, S, D = h.shape
    nb = (Bn * S) // BLOCK
    hb = h.reshape(nb, BLOCK, D)

    def one_block(xb):
        q = (xb @ w_q).reshape(BLOCK, PEER_HEADS, 2, PEER_HALF)
        s = jnp.einsum('thcd,hcnd->thcn', q, sub_keys).astype(jnp.float32)
        top_s, top_i = lax.top_k(s, PEER_TOPK)
        cand_s = top_s[:, :, 0, :, None] + top_s[:, :, 1, None, :]
        cand_i = top_i[:, :, 0, :, None] * PEER_N_KEYS + top_i[:, :, 1, None, :]
        best_s, best_j = lax.top_k(cand_s.reshape(BLOCK, PEER_HEADS, -1), PEER_TOPK)
        eidx = jnp.take_along_axis(cand_i.reshape(BLOCK, PEER_HEADS, -1), best_j, axis=-1)
        g = jax.nn.softmax(best_s, axis=-1)
        ue = u[eidx]
        ve = v[eidx]
        act = jax.nn.gelu(jnp.einsum('thkd,td->thk', ue, xb).astype(jnp.float32), approximate=False)
        return jnp.einsum('thk,thkd->td', (g * act).astype(ve.dtype), ve)

    return lax.map(one_block, hb).reshape(Bn, S, D)


def setup_inputs(seed: int = 0) -> dict:
    key = jax.random.key(seed)
    ks = jax.random.split(key, 24)
    f32 = jnp.float32

    def nrm(k, shape, scale):
        return jax.random.normal(k, shape, f32) * scale

    def gain(k, n):
        return 1.0 + 0.02 * jax.random.normal(k, (n,), f32)

    return {
        'x': jax.random.normal(ks[0], (BATCH, SEQ, D_MODEL), f32),
        'positions': jnp.broadcast_to(jnp.arange(SEQ, dtype=jnp.int32), (BATCH, SEQ)),
        'norm_mix0': gain(ks[1], D_MODEL),
        'w_in0': nrm(ks[2], (D_MODEL, IN0_COLS), D_MODEL ** -0.5),
        'sinks0': nrm(ks[3], (A_Q_HEADS,), 0.5),
        'conv_w0': nrm(ks[4], (CONV_WIDTH, B_WIDTH), CONV_WIDTH ** -0.5),
        'w_out0': nrm(ks[5], (MIX0_WIDTH, D_MODEL), MIX0_WIDTH ** -0.5),
        'norm_ffn0': gain(ks[6], D_MODEL),
        'peer_wq0': nrm(ks[7], (D_MODEL, PEER_HEADS * PEER_QDIM), D_MODEL ** -0.5),
        'peer_keys0': nrm(ks[8], (PEER_HEADS, 2, PEER_N_KEYS, PEER_HALF), PEER_HALF ** -0.5),
        'peer_u0': nrm(ks[9], (PEER_EXPERTS, D_MODEL), D_MODEL ** -0.5),
        'peer_v0': nrm(ks[10], (PEER_EXPERTS, D_MODEL), 0.25),
        'norm_mix1': gain(ks[11], D_MODEL),
        'w_in1': nrm(ks[12], (D_MODEL, IN1_COLS), D_MODEL ** -0.5),
        'g_qa1': gain(ks[13], C_Q_RANK),
        'w_uq1': nrm(ks[14], (C_Q_RANK, UQ_COLS), C_Q_RANK ** -0.5),
        'w_out1': nrm(ks[15], (MIX1_WIDTH, D_MODEL), MIX1_WIDTH ** -0.5),
        'norm_ffn1': gain(ks[16], D_MODEL),
        'peer_wq1': nrm(ks[17], (D_MODEL, PEER_HEADS * PEER_QDIM), D_MODEL ** -0.5),
        'peer_keys1': nrm(ks[18], (PEER_HEADS, 2, PEER_N_KEYS, PEER_HALF), PEER_HALF ** -0.5),
        'peer_u1': nrm(ks[19], (PEER_EXPERTS, D_MODEL), D_MODEL ** -0.5),
        'peer_v1': nrm(ks[20], (PEER_EXPERTS, D_MODEL), 0.25),
        'norm_final': gain(ks[21], D_MODEL),
    }


def reference(x, positions,
              norm_mix0, w_in0, sinks0, conv_w0, w_out0,
              norm_ffn0, peer_wq0, peer_keys0, peer_u0, peer_v0,
              norm_mix1, w_in1, g_qa1, w_uq1, w_out1,
              norm_ffn1, peer_wq1, peer_keys1, peer_u1, peer_v1,
              norm_final):
    mixer_params = [(norm_mix0, w_in0, sinks0, conv_w0, w_out0),
                    (norm_mix1, w_in1, g_qa1, w_uq1, w_out1)]
    ffn_params = [(norm_ffn0, peer_wq0, peer_keys0, peer_u0, peer_v0),
                  (norm_ffn1, peer_wq1, peer_keys1, peer_u1, peer_v1)]
    for layer in range(DEPTH):
        if layer % 2 == 0:
            x = even_mixer(x, positions, *mixer_params[layer])
        else:
            x = odd_mixer(x, positions, *mixer_params[layer])
        g_ffn, wq, keys, u, v = ffn_params[layer]
        x = x + peer_ffn(rms_norm(x, g_ffn), wq, keys, u, v)
    return rms_norm(x, norm_final)
```

```python
import functools

import numpy as np
import jax
import jax.numpy as jnp
from jax import lax
from jax.experimental import pallas as pl
from jax.experimental.pallas import tpu as pltpu

F32 = jnp.float32
BF16 = jnp.bfloat16
MXU_DTYPE = jnp.bfloat16

LANES = 128
EPS = 1e-5
ROPE_THETA = 500000.0
BLOCK = 128
NEG_INF = -1e30
INT_MIN = -(2 ** 31)

SWA_WINDOW = 128
A_Q_HEADS, A_KV_HEADS, A_HEAD_DIM = 16, 4, 64
A_Q_COLS = A_Q_HEADS * A_HEAD_DIM
A_KV_COLS = A_KV_HEADS * A_HEAD_DIM
B_WIDTH = 1024
C_Q_RANK, C_HEADS, C_HEAD_DIM = 512, 16, 128
IDX_HEADS, IDX_HEAD_DIM, INDEX_TOPK = 16, 64, 256
IN1_COLS_PADDED = 896
PEER_HEADS, PEER_N_KEYS, PEER_TOPK, PEER_HALF = 8, 128, 16, 128
PEER_EXPERTS = PEER_N_KEYS * PEER_N_KEYS

VMEM_LIMIT = 56 * 1024 * 1024


def _cparams(*sem):
    return pltpu.CompilerParams(dimension_semantics=sem, vmem_limit_bytes=VMEM_LIMIT)


def _dot(a, b):
    return jnp.dot(a.astype(MXU_DTYPE), b.astype(MXU_DTYPE), preferred_element_type=F32)


def _dot_nt(a, b):
    return lax.dot_general(a.astype(MXU_DTYPE), b.astype(MXU_DTYPE),
                           (((1,), (1,)), ((), ())), preferred_element_type=F32)


def _rope_table(pos, heads, half):
    rot = 2 * half
    inv = ROPE_THETA ** (-jnp.arange(half, dtype=F32) * 2.0 / rot)
    ang = pos.astype(F32)[:, None] * inv
    cos, sin = jnp.cos(ang), jnp.sin(ang)
    t = pos.shape[0]
    c = jnp.ones((t, LANES), F32)
    s1 = jnp.zeros((t, LANES), F32)
    s2 = jnp.zeros((t, LANES), F32)
    for lane0, dh in heads:
        assert dh // 8 == half
        c = c.at[:, lane0:lane0 + half].set(cos).at[:, lane0 + half:lane0 + rot].set(cos)
        s1 = s1.at[:, lane0 + half:lane0 + rot].set(sin)
        s2 = s2.at[:, lane0:lane0 + half].set(-sin)
    return jnp.stack([c, s1, s2])


def _rope_chunk(xc, tab_ref, half):
    return (xc * tab_ref[0] + pltpu.roll(xc, half, 1) * tab_ref[1]
            + pltpu.roll(xc, LANES - half, 1) * tab_ref[2])


def _proj_kernel(*refs, n_tabs, halves, groups, tn, with_xn):
    x_ref, g_ref, w_ref = refs[:3]
    tab_refs = refs[3:3 + n_tabs]
    o_ref = refs[3 + n_tabs]
    xn_out = refs[4 + n_tabs] if with_xn else None
    xn_ref = refs[-1]
    j = pl.program_id(1)

    @pl.when(j == 0)
    def _():
        x = x_ref[...]
        ms = jnp.mean(x * x, axis=-1, keepdims=True)
        xn = (x * lax.rsqrt(ms + EPS) * g_ref[...]).astype(MXU_DTYPE)
        xn_ref[...] = xn
        if with_xn:
            xn_out[...] = xn

    acc = jnp.dot(xn_ref[...], w_ref[...], preferred_element_type=F32)

    for (j_lo, j_hi), cfg in groups:
        @pl.when((j >= j_lo) & (j < j_hi))
        def _(cfg=cfg):
            for c in range(tn // LANES):
                xc = acc[:, c * LANES:(c + 1) * LANES]
                if cfg[c] >= 0:
                    xc = _rope_chunk(xc, tab_refs[cfg[c]], halves[cfg[c]])
                o_ref[:, c * LANES:(c + 1) * LANES] = xc


def _proj(x, g, w, *, k_cols, tm, tn, tabs=(), halves=(), groups=None, with_xn=False):
    t = x.shape[0]
    n = w.shape[1]
    nj = n // tn
    if groups is None:
        groups = (((0, nj), (-1,) * (tn // LANES)),)
    kern = functools.partial(_proj_kernel, n_tabs=len(tabs), halves=tuple(halves),
                             groups=tuple(groups), tn=tn, with_xn=with_xn)
    in_specs = [pl.BlockSpec((tm, k_cols), lambda i, j: (i, 0)),
                pl.BlockSpec((1, k_cols), lambda i, j: (0, 0)),
                pl.BlockSpec((k_cols, tn), lambda i, j: (0, j))]
    in_specs += [pl.BlockSpec((3, tm, LANES), lambda i, j: (0, i, 0)) for _ in tabs]
    out_shape = [jax.ShapeDtypeStruct((t, n), F32)]
    out_specs = [pl.BlockSpec((tm, tn), lambda i, j: (i, j))]
    if with_xn:
        out_shape.append(jax.ShapeDtypeStruct((t, k_cols), MXU_DTYPE))
        out_specs.append(pl.BlockSpec((tm, k_cols), lambda i, j: (i, 0)))
    res = pl.pallas_call(
        kern, grid=(t // tm, nj), in_specs=in_specs, out_specs=out_specs, out_shape=out_shape,
        scratch_shapes=[pltpu.VMEM((tm, k_cols), MXU_DTYPE)],
        compiler_params=_cparams("parallel", "arbitrary"), name="norm_proj",
    )(x, g.reshape(1, k_cols), w, *tabs)
    return res if with_xn else res[0]


def _mm_res_kernel(*refs, n_lhs):
    a_refs = refs[:n_lhs]
    w_refs = refs[n_lhs:2 * n_lhs]
    x_ref = refs[2 * n_lhs]
    o_ref = refs[2 * n_lhs + 1]
    ab_refs = refs[2 * n_lhs + 2:]

    @pl.when(pl.program_id(1) == 0)
    def _():
        for a_ref, ab_ref in zip(a_refs, ab_refs):
            ab_ref[...] = a_ref[...].astype(MXU_DTYPE)

    acc = x_ref[...]
    for ab_ref, w_ref in zip(ab_refs, w_refs):
        acc = acc + jnp.dot(ab_ref[...], w_ref[...], preferred_element_type=F32)
    o_ref[...] = acc


def _mm_res(lhs, ws, x, *, tm, tn):
    t, d = x.shape
    n_lhs = len(lhs)
    in_specs = [pl.BlockSpec((tm, a.shape[1]), lambda i, j: (i, 0)) for a in lhs]
    in_specs += [pl.BlockSpec((w.shape[0], tn), lambda i, j: (0, j)) for w in ws]
    in_specs += [pl.BlockSpec((tm, tn), lambda i, j: (i, j))]
    return pl.pallas_call(
        functools.partial(_mm_res_kernel, n_lhs=n_lhs), grid=(t // tm, d // tn),
        in_specs=in_specs, out_specs=pl.BlockSpec((tm, tn), lambda i, j: (i, j)),
        out_shape=jax.ShapeDtypeStruct((t, d), F32),
        scratch_shapes=[pltpu.VMEM((tm, a.shape[1]), MXU_DTYPE) for a in lhs],
        compiler_params=_cparams("parallel", "arbitrary"), name="out_proj",
    )(*lhs, *ws, x)


def _swa_kernel(sink_ref, q_ref, kp_ref, kc_ref, vp_ref, vc_ref, o_ref):
    n = pl.program_id(1)
    g_sz = A_Q_HEADS // A_KV_HEADS
    dh = A_HEAD_DIM
    q = q_ref[...]
    k2 = jnp.concatenate([kp_ref[...], kc_ref[...]], axis=0).astype(MXU_DTYPE)
    v2 = jnp.concatenate([vp_ref[...], vc_ref[...]], axis=0).astype(MXU_DTYPE)
    qi = lax.broadcasted_iota(jnp.int32, (BLOCK, 2 * BLOCK), 0)
    kj = lax.broadcasted_iota(jnp.int32, (BLOCK, 2 * BLOCK), 1)
    diff = qi + BLOCK - kj
    valid = (diff >= 0) & (diff < SWA_WINDOW) & ((kj >= BLOCK) | (n > 0))
    scale = dh ** -0.5
    for h in range(A_KV_HEADS):
        kh = k2[:, h * dh:(h + 1) * dh]
        vh = v2[:, h * dh:(h + 1) * dh]
        for g in range(g_sz):
            hq = h * g_sz + g
            s = _dot_nt(q[:, hq * dh:(hq + 1) * dh], kh) * scale
            s = jnp.where(valid, s, NEG_INF)
            sink = sink_ref[hq]
            m = jnp.maximum(jnp.max(s, axis=-1, keepdims=True), sink)
            p = jnp.exp(s - m)
            denom = jnp.sum(p, axis=-1, keepdims=True) + jnp.exp(sink - m)
            o_ref[:, hq * dh:(hq + 1) * dh] = _dot(p, vh) / denom


def _swa(p0, sinks, b, s):
    nb = s // BLOCK
    kv_w = A_KV_COLS
    k_blk = A_Q_COLS // kv_w
    v_blk = k_blk + 1
    cur = lambda bi, n: bi * nb + n
    prev = lambda bi, n: bi * nb + jnp.maximum(n - 1, 0)
    return pl.pallas_call(
        _swa_kernel, grid=(b, nb),
        in_specs=[pl.BlockSpec(memory_space=pltpu.SMEM),
                  pl.BlockSpec((BLOCK, A_Q_COLS), lambda bi, n: (cur(bi, n), 0)),
                  pl.BlockSpec((BLOCK, kv_w), lambda bi, n: (prev(bi, n), k_blk)),
                  pl.BlockSpec((BLOCK, kv_w), lambda bi, n: (cur(bi, n), k_blk)),
                  pl.BlockSpec((BLOCK, kv_w), lambda bi, n: (prev(bi, n), v_blk)),
                  pl.BlockSpec((BLOCK, kv_w), lambda bi, n: (cur(bi, n), v_blk))],
        out_specs=pl.BlockSpec((BLOCK, A_Q_COLS), lambda bi, n: (cur(bi, n), 0)),
        out_shape=jax.ShapeDtypeStruct((b * s, A_Q_COLS), F32),
        compiler_params=_cparams("parallel", "arbitrary"), name="swa",
    )(sinks, p0, p0, p0, p0, p0)


CONV_ROWS = 256
CONV_COLS = 512
HALO = 8


def _conv_kernel(bg_ref, cg_ref, hx_ref, cgh_ref, hxh_ref, w_ref, o_ref, *, tiles_per_seq):
    i = pl.program_id(0)
    z = cg_ref[...] * hx_ref[...]
    first = (i % tiles_per_seq) == 0
    zh = jnp.where(first, 0.0, cgh_ref[...] * hxh_ref[...])
    row = lax.broadcasted_iota(jnp.int32, z.shape, 0)
    z1 = pltpu.roll(z, 1, 0)
    z1 = jnp.where(row == 0, zh[HALO - 1:HALO], z1)
    z2 = pltpu.roll(z, 2, 0)
    z2 = jnp.where(row == 0, zh[HALO - 2:HALO - 1], jnp.where(row == 1, zh[HALO - 1:HALO], z2))
    w = w_ref[...]
    o_ref[...] = bg_ref[...] * (w[0:1] * z2 + w[1:2] * z1 + w[2:3] * z)


def _conv(p0, conv_w, s):
    t = p0.shape[0]
    col0 = A_Q_COLS + 2 * A_KV_COLS
    bg_blk, cg_blk, hx_blk = (col0 // CONV_COLS, (col0 + B_WIDTH) // CONV_COLS,
                              (col0 + 2 * B_WIDTH) // CONV_COLS)
    rh = CONV_ROWS // HALO
    halo_row = lambda i: jnp.maximum(i * rh - 1, 0)
    return pl.pallas_call(
        functools.partial(_conv_kernel, tiles_per_seq=s // CONV_ROWS),
        grid=(t // CONV_ROWS, B_WIDTH // CONV_COLS),
        in_specs=[pl.BlockSpec((CONV_ROWS, CONV_COLS), lambda i, j: (i, bg_blk + j)),
                  pl.BlockSpec((CONV_ROWS, CONV_COLS), lambda i, j: (i, cg_blk + j)),
                  pl.BlockSpec((CONV_ROWS, CONV_COLS), lambda i, j: (i, hx_blk + j)),
                  pl.BlockSpec((HALO, CONV_COLS), lambda i, j: (halo_row(i), cg_blk + j)),
                  pl.BlockSpec((HALO, CONV_COLS), lambda i, j: (halo_row(i), hx_blk + j)),
                  pl.BlockSpec((3, CONV_COLS), lambda i, j: (0, j))],
        out_specs=pl.BlockSpec((CONV_ROWS, CONV_COLS), lambda i, j: (i, j)),
        out_shape=jax.ShapeDtypeStruct((t, B_WIDTH), F32),
        compiler_params=_cparams("parallel", "parallel"), name="gated_conv",
    )(p0, p0, p0, p0, p0, conv_w)


def _dsa_kernel(qa_ref, qb_ref, iq_ref, iwq_ref, k_ref, v_ref, ikw_ref, o_ref, key_ref, *, k_sel):
    n = pl.program_id(1)
    s_len = k_ref.shape[0]
    ik = ikw_ref[:, 0:IDX_HEAD_DIM].astype(MXU_DTYPE)
    iw = iwq_ref[:, IDX_HEAD_DIM:IDX_HEAD_DIM + IDX_HEADS] * (
        (IDX_HEADS ** -0.5) * (IDX_HEAD_DIM ** -0.5))
    iq = iq_ref[...]
    score = jnp.zeros((BLOCK, s_len), F32)
    for h in range(IDX_HEADS):
        lg = _dot_nt(iq[:, h * IDX_HEAD_DIM:(h + 1) * IDX_HEAD_DIM], ik)
        score = score + iw[:, h:h + 1] * jnp.maximum(lg, 0.0)
    score = score + 0.0
    qpos = n * BLOCK + lax.broadcasted_iota(jnp.int32, (BLOCK, s_len), 0)
    kpos = lax.broadcasted_iota(jnp.int32, (BLOCK, s_len), 1)
    causal = kpos <= qpos
    bits = pltpu.bitcast(score, jnp.int32)
    key = jnp.where(bits < 0, bits ^ jnp.int32(0x7FFFFFFF), bits)
    key_ref[...] = jnp.where(causal, key, INT_MIN)

    def count_ge(thr):
        return jnp.sum(jnp.where(key_ref[...] >= thr, 1.0, 0.0), axis=-1, keepdims=True)

    kf = float(k_sel)
    thr0 = jnp.where(count_ge(jnp.zeros((BLOCK, 1), jnp.int32)) >= kf,
                     jnp.int32(0), jnp.int32(INT_MIN))

    def bit_step(it, thr):
        cand = thr + jnp.left_shift(jnp.int32(1), jnp.int32(30) - it)
        return jnp.where(count_ge(cand) >= kf, cand, thr)

    thr = lax.fori_loop(0, 31, bit_step, thr0)
    thr = jnp.maximum(thr, INT_MIN + 1)
    sel = key_ref[...] >= thr

    kk = k_ref[...].astype(MXU_DTYPE)
    vv = v_ref[...].astype(MXU_DTYPE)
    scale = C_HEAD_DIM ** -0.5
    for h in range(C_HEADS):
        q_r = qa_ref if h < C_HEADS // 2 else qb_ref
        hh = h % (C_HEADS // 2)
        s = _dot_nt(q_r[:, hh * C_HEAD_DIM:(hh + 1) * C_HEAD_DIM], kk) * scale
        s = jnp.where(sel, s, NEG_INF)
        m = jnp.max(s, axis=-1, keepdims=True)
        p = jnp.exp(s - m)
        denom = jnp.sum(p, axis=-1, keepdims=True)
        o_ref[:, h * C_HEAD_DIM:(h + 1) * C_HEAD_DIM] = _dot(p, vv) / denom


def _dsa(u, p1, b, s):
    nb = s // BLOCK
    k_sel = min(INDEX_TOPK, s // 4)
    half_q = C_HEADS * C_HEAD_DIM // 2
    k_blk = C_Q_RANK // LANES
    row = lambda bi, n: bi * nb + n
    return pl.pallas_call(
        functools.partial(_dsa_kernel, k_sel=k_sel), grid=(b, nb),
        in_specs=[pl.BlockSpec((BLOCK, half_q), lambda bi, n: (row(bi, n), 0)),
                  pl.BlockSpec((BLOCK, half_q), lambda bi, n: (row(bi, n), 1)),
                  pl.BlockSpec((BLOCK, half_q), lambda bi, n: (row(bi, n), 2)),
                  pl.BlockSpec((BLOCK, LANES), lambda bi, n: (row(bi, n), k_blk + 2)),
                  pl.BlockSpec((s, LANES), lambda bi, n: (bi, k_blk)),
                  pl.BlockSpec((s, LANES), lambda bi, n: (bi, k_blk + 1)),
                  pl.BlockSpec((s, LANES), lambda bi, n: (bi, k_blk + 2))],
        out_specs=pl.BlockSpec((BLOCK, C_HEADS * C_HEAD_DIM), lambda bi, n: (row(bi, n), 0)),
        out_shape=jax.ShapeDtypeStruct((b * s, C_HEADS * C_HEAD_DIM), F32),
        scratch_shapes=[pltpu.VMEM((BLOCK, s), jnp.int32)],
        compiler_params=_cparams("parallel", "arbitrary"), name="dsa",
    )(u, u, u, p1, p1, p1, p1)


ROUTER_TOKENS = 256
N_FULL_K0 = 8
N_CAND = N_FULL_K0 * PEER_TOPK + (PEER_TOPK - N_FULL_K0)


def _top_rows(work, k):
    rows, cols = work.shape
    row = lax.broadcasted_iota(jnp.int32, (rows, cols), 0).astype(F32)
    vrow = lax.broadcasted_iota(jnp.int32, (k, cols), 0)
    rank = jnp.full((rows, cols), float(k), F32)
    vals = jnp.zeros((k, cols), F32)
    for r in range(k):
        m = jnp.max(work, axis=0, keepdims=True)
        first = jnp.min(jnp.where(work == m, row, float(rows)), axis=0, keepdims=True)
        hit = row == first
        rank = jnp.where(hit, float(r), rank)
        work = jnp.where(hit, -jnp.inf, work)
        vals = jnp.where(vrow == r, m, vals)
    return rank, vals


def _router_kernel(q_ref, keys_ref, p0_ref, n0_ref, p1_ref, r1_ref):
    k = PEER_TOPK

    def head(h, carry):
        col0 = pl.multiple_of(h * (2 * PEER_HALF), LANES)
        col1 = pl.multiple_of(col0 + PEER_HALF, LANES)
        s0 = _dot_nt(keys_ref[2 * h], q_ref[:, pl.ds(col0, PEER_HALF)])
        s1 = _dot_nt(keys_ref[2 * h + 1], q_ref[:, pl.ds(col1, PEER_HALF)])
        rank0, v0 = _top_rows(s0, k)
        rank1, v1 = _top_rows(s1, k)
        cand = jnp.concatenate([v0[k0:k0 + 1] + v1 for k0 in range(N_FULL_K0)]
                               + [v0[N_FULL_K0:] + v1[0:1]], axis=0)
        rank_c, _ = _top_rows(cand, k)
        sel = rank_c < float(k)
        cnt = jnp.where(sel, 1.0, 0.0)
        n_rank = jnp.concatenate(
            [jnp.sum(cnt[k0 * k:(k0 + 1) * k], axis=0, keepdims=True) for k0 in range(N_FULL_K0)]
            + [cnt[N_FULL_K0 * k:]], axis=0)
        pc = jnp.exp(cand - cand[0:1])
        z = jnp.sum(jnp.where(sel, pc, 0.0), axis=0, keepdims=True)
        n0 = jnp.zeros_like(s0)
        for r in range(k):
            n0 = jnp.where(rank0 == float(r), n_rank[r:r + 1], n0)
        p0_ref[h] = jnp.where(rank0 < float(k), jnp.exp(s0 - v0[0:1]), 0.0) / z
        n0_ref[h] = n0
        p1_ref[h] = jnp.where(rank1 < float(k), jnp.exp(s1 - v1[0:1]), 0.0).astype(p1_ref.dtype)
        r1_ref[h] = rank1.astype(r1_ref.dtype)
        return carry

    lax.fori_loop(0, PEER_HEADS, head, 0)


def _router(q, keys):
    t = q.shape[0]
    tb = ROUTER_TOKENS
    blk = pl.BlockSpec((PEER_HEADS, PEER_N_KEYS, tb), lambda i: (0, 0, i))
    f32s = jax.ShapeDtypeStruct((PEER_HEADS, PEER_N_KEYS, t), F32)
    bf16s = jax.ShapeDtypeStruct((PEER_HEADS, PEER_N_KEYS, t), BF16)
    return pl.pallas_call(
        _router_kernel, grid=(t // tb,),
        in_specs=[pl.BlockSpec((tb, q.shape[1]), lambda i: (i, 0)),
                  pl.BlockSpec(keys.shape, lambda i: (0, 0, 0))],
        out_specs=[blk, blk, blk, blk], out_shape=[f32s, f32s, bf16s, bf16s],
        compiler_params=_cparams("parallel"), name="peer_router",
    )(q, keys)


EXPERT_TOKENS = 512
EXPERT_CHUNK = 1024
ROWS_PER_CHUNK = EXPERT_CHUNK // PEER_N_KEYS


def _expert_kernel(xn_ref, x_ref, u_ref, vt_ref, p0_ref, n0_ref, p1_ref, r1_ref, o_ref, acc_ref,
                   h_ref):
    c = pl.program_id(1)

    @pl.when(c == 0)
    def _():
        acc_ref[...] = jnp.zeros_like(acc_ref)

    pre = _dot_nt(u_ref[...], xn_ref[...])
    act = 0.5 * pre * (1.0 + lax.erf(pre * np.sqrt(0.5).astype(np.float32)))
    for ii in range(ROWS_PER_CHUNK):
        w = None
        for h in range(PEER_HEADS):
            gate = p0_ref[h, ii:ii + 1, :].astype(BF16)
            cnt = n0_ref[h, ii:ii + 1, :].astype(BF16)
            term = jnp.where(r1_ref[h] < cnt, p1_ref[h], jnp.zeros((), BF16)) * gate
            w = term if w is None else w + term
        rows = slice(ii * PEER_N_KEYS, (ii + 1) * PEER_N_KEYS)
        h_ref[rows, :] = act[rows, :].astype(BF16) * w
    acc_ref[...] += jnp.dot(vt_ref[...], h_ref[...], preferred_element_type=F32)

    @pl.when(c == pl.num_programs(1) - 1)
    def _():
        o_ref[...] = x_ref[...] + acc_ref[...].T


def _experts(xn, x, u, vt, p0, n0, p1, r1):
    t, d = x.shape
    tb, ec = EXPERT_TOKENS, EXPERT_CHUNK
    small = pl.BlockSpec((PEER_HEADS, ROWS_PER_CHUNK, tb), lambda i, c: (0, c, i))
    full = pl.BlockSpec((PEER_HEADS, PEER_N_KEYS, tb), lambda i, c: (0, 0, i))
    return pl.pallas_call(
        _expert_kernel, grid=(t // tb, PEER_EXPERTS // ec),
        in_specs=[pl.BlockSpec((tb, d), lambda i, c: (i, 0)),
                  pl.BlockSpec((tb, d), lambda i, c: (i, 0)),
                  pl.BlockSpec((ec, d), lambda i, c: (c, 0)),
                  pl.BlockSpec((d, ec), lambda i, c: (0, c)),
                  small, small, full, full],
        out_specs=pl.BlockSpec((tb, d), lambda i, c: (i, 0)),
        out_shape=jax.ShapeDtypeStruct((t, d), F32),
        scratch_shapes=[pltpu.VMEM((d, tb), F32), pltpu.VMEM((ec, tb), BF16)],
        compiler_params=_cparams("parallel", "arbitrary"), name="peer_experts",
    )(xn, x, u, vt, p0, n0, p1, r1)


def _peer(x, g, wq, keys, u, v):
    xq, xn = _proj(x, g, wq.astype(MXU_DTYPE), k_cols=x.shape[1], tm=512, tn=512, with_xn=True)
    keys2 = keys.reshape(2 * PEER_HEADS, PEER_N_KEYS, PEER_HALF).astype(MXU_DTYPE)
    p0, n0, p1, r1 = _router(xq, keys2)
    return _experts(xn, x, u.astype(MXU_DTYPE), v.astype(MXU_DTYPE).T, p0, n0, p1, r1)


def _norm_kernel(x_ref, g_ref, o_ref):
    x = x_ref[...]
    ms = jnp.mean(x * x, axis=-1, keepdims=True)
    o_ref[...] = x * lax.rsqrt(ms + EPS) * g_ref[...]


def _norm(x, g, tm=512):
    t, d = x.shape
    return pl.pallas_call(
        _norm_kernel, grid=(t // tm,),
        in_specs=[pl.BlockSpec((tm, d), lambda i: (i, 0)), pl.BlockSpec((1, d), lambda i: (0, 0))],
        out_specs=pl.BlockSpec((tm, d), lambda i: (i, 0)),
        out_shape=jax.ShapeDtypeStruct((t, d), F32),
        compiler_params=_cparams("parallel"), name="final_norm",
    )(x, g.reshape(1, d))


def kernel(x, positions, norm_mix0, w_in0, sinks0, conv_w0, w_out0, norm_ffn0, peer_wq0, peer_keys0,
           peer_u0, peer_v0, norm_mix1, w_in1, g_qa1, w_uq1, w_out1, norm_ffn1, peer_wq1,
           peer_keys1, peer_u1, peer_v1, norm_final):
    b, s, d = x.shape
    t = b * s
    xf = x.reshape(t, d)
    pos = positions.reshape(t)
    tab_a = _rope_table(pos, ((0, 64), (64, 64)), 8)
    tab_b = _rope_table(pos, ((0, 128),), 16)
    tab_c = _rope_table(pos, ((0, 64),), 8)

    tn0 = 256
    n_rope = (A_Q_COLS + A_KV_COLS) // tn0
    groups0 = (((0, n_rope), (0, 0)), ((n_rope, w_in0.shape[1] // tn0), (-1, -1)))
    p0 = _proj(xf, norm_mix0, w_in0.astype(MXU_DTYPE), k_cols=d, tm=512, tn=tn0,
               tabs=(tab_a,), halves=(8,), groups=groups0)
    a_out = _swa(p0, sinks0, b, s)
    b_out = _conv(p0, conv_w0, s)
    w_o = w_out0.astype(MXU_DTYPE)
    x1 = _mm_res([a_out, b_out], [w_o[:A_Q_COLS], w_o[A_Q_COLS:]], xf, tm=512, tn=512)
    x2 = _peer(x1, norm_ffn0, peer_wq0, peer_keys0, peer_u0, peer_v0)

    w1 = jnp.pad(w_in1, ((0, 0), (0, IN1_COLS_PADDED - w_in1.shape[1]))).astype(MXU_DTYPE)
    k_chunk = C_Q_RANK // LANES
    cfg1 = tuple({k_chunk: 0, k_chunk + 2: 1}.get(c, -1) for c in range(IN1_COLS_PADDED // LANES))
    p1 = _proj(x2, norm_mix1, w1, k_cols=d, tm=512, tn=IN1_COLS_PADDED,
               tabs=(tab_b, tab_c), halves=(16, 8), groups=(((0, 1), cfg1),))
    n_q = C_HEADS * C_HEAD_DIM // LANES
    n_iq = IDX_HEADS * IDX_HEAD_DIM // LANES
    cfg_u = (0,) * n_q + (1,) * n_iq
    u = _proj(p1, g_qa1, w_uq1.astype(MXU_DTYPE), k_cols=C_Q_RANK, tm=256,
              tn=(n_q + n_iq) * LANES, tabs=(tab_b, tab_a), halves=(16, 8),
              groups=(((0, 1), cfg_u),))
    o = _dsa(u, p1, b, s)
    x3 = _mm_res([o], [w_out1.astype(MXU_DTYPE)], x2, tm=512, tn=512)
    x4 = _peer(x3, norm_ffn1, peer_wq1, peer_keys1, peer_u1, peer_v1)
    return _norm(x4, norm_final).reshape(b, s, d)
```

```python
import functools

import numpy as np
import jax
import jax.numpy as jnp
from jax import lax
from jax.experimental import pallas as pl
from jax.experimental.pallas import tpu as pltpu

F32 = jnp.float32
BF16 = jnp.bfloat16
MXU_DTYPE = jnp.bfloat16

LANES = 128
EPS = 1e-5
ROPE_THETA = 500000.0
BLOCK = 128
NEG_INF = -1e30
INT_MIN = -(2 ** 31)

SWA_WINDOW = 128
A_Q_HEADS, A_KV_HEADS, A_HEAD_DIM = 16, 4, 64
A_Q_COLS = A_Q_HEADS * A_HEAD_DIM
A_KV_COLS = A_KV_HEADS * A_HEAD_DIM
B_WIDTH = 1024
C_Q_RANK, C_HEADS, C_HEAD_DIM = 512, 16, 128
IDX_HEADS, IDX_HEAD_DIM, INDEX_TOPK = 16, 64, 256
IN1_COLS_PADDED = 896
PEER_HEADS, PEER_N_KEYS, PEER_TOPK, PEER_HALF = 8, 128, 16, 128
PEER_EXPERTS = PEER_N_KEYS * PEER_N_KEYS

VMEM_LIMIT = 56 * 1024 * 1024


def _cparams(*sem):
    return pltpu.CompilerParams(dimension_semantics=sem, vmem_limit_bytes=VMEM_LIMIT)


def _dot(a, b):
    return jnp.dot(a.astype(MXU_DTYPE), b.astype(MXU_DTYPE), preferred_element_type=F32)


def _dot_nt(a, b):
    return lax.dot_general(a.astype(MXU_DTYPE), b.astype(MXU_DTYPE),
                           (((1,), (1,)), ((), ())), preferred_element_type=F32)


def _rope_table(pos, heads, half):
    rot = 2 * half
    inv = ROPE_THETA ** (-jnp.arange(half, dtype=F32) * 2.0 / rot)
    freq = np.zeros(LANES, np.int32)
    rotated = np.zeros(LANES, bool)
    upper = np.zeros(LANES, bool)
    for lane0, dh in heads:
        assert dh // 8 == half
        freq[lane0:lane0 + rot] = np.arange(rot) % half
        rotated[lane0:lane0 + rot] = True
        upper[lane0 + half:lane0 + rot] = True
    inv_lane = jnp.where(rotated, inv[freq], 0.0)
    ang = pos.astype(F32)[:, None] * inv_lane[None, :]
    cos, sin = jnp.cos(ang), jnp.sin(ang)
    s1 = jnp.where(upper, sin, 0.0)
    s2 = jnp.where(rotated & ~upper, -sin, 0.0)
    return jnp.stack([cos, s1, s2])


def _rope_chunk(xc, tab_ref, half):
    return (xc * tab_ref[0] + pltpu.roll(xc, half, 1) * tab_ref[1]
            + pltpu.roll(xc, LANES - half, 1) * tab_ref[2])


def _proj_kernel(*refs, n_tabs, halves, groups, tn, with_xn):
    x_ref, g_ref, w_ref = refs[:3]
    tab_refs = refs[3:3 + n_tabs]
    o_ref = refs[3 + n_tabs]
    xn_out = refs[4 + n_tabs] if with_xn else None
    xn_ref = refs[-1]
    j = pl.program_id(1)

    @pl.when(j == 0)
    def _():
        x = x_ref[...]
        ms = jnp.mean(x * x, axis=-1, keepdims=True)
        xn = (x * lax.rsqrt(ms + EPS) * g_ref[...]).astype(MXU_DTYPE)
        xn_ref[...] = xn
        if with_xn:
            xn_out[...] = xn

    acc = jnp.dot(xn_ref[...], w_ref[...], preferred_element_type=F32)

    for (j_lo, j_hi), cfg in groups:
        @pl.when((j >= j_lo) & (j < j_hi))
        def _(cfg=cfg):
            for c in range(tn // LANES):
                xc = acc[:, c * LANES:(c + 1) * LANES]
                if cfg[c] >= 0:
                    xc = _rope_chunk(xc, tab_refs[cfg[c]], halves[cfg[c]])
                o_ref[:, c * LANES:(c + 1) * LANES] = xc


def _proj(x, g, w, *, k_cols, tm, tn, tabs=(), halves=(), groups=None, with_xn=False):
    t = x.shape[0]
    n = w.shape[1]
    nj = n // tn
    if groups is None:
        groups = (((0, nj), (-1,) * (tn // LANES)),)
    kern = functools.partial(_proj_kernel, n_tabs=len(tabs), halves=tuple(halves),
                             groups=tuple(groups), tn=tn, with_xn=with_xn)
    in_specs = [pl.BlockSpec((tm, k_cols), lambda i, j: (i, 0)),
                pl.BlockSpec((1, k_cols), lambda i, j: (0, 0)),
                pl.BlockSpec((k_cols, tn), lambda i, j: (0, j))]
    in_specs += [pl.BlockSpec((3, tm, LANES), lambda i, j: (0, i, 0)) for _ in tabs]
    out_shape = [jax.ShapeDtypeStruct((t, n), F32)]
    out_specs = [pl.BlockSpec((tm, tn), lambda i, j: (i, j))]
    if with_xn:
        out_shape.append(jax.ShapeDtypeStruct((t, k_cols), MXU_DTYPE))
        out_specs.append(pl.BlockSpec((tm, k_cols), lambda i, j: (i, 0)))
    res = pl.pallas_call(
        kern, grid=(t // tm, nj), in_specs=in_specs, out_specs=out_specs, out_shape=out_shape,
        scratch_shapes=[pltpu.VMEM((tm, k_cols), MXU_DTYPE)],
        compiler_params=_cparams("parallel", "arbitrary"), name="norm_proj",
    )(x, g.reshape(1, k_cols), w, *tabs)
    return res if with_xn else res[0]


def _mm_res_kernel(*refs, n_lhs):
    a_refs = refs[:n_lhs]
    w_refs = refs[n_lhs:2 * n_lhs]
    x_ref = refs[2 * n_lhs]
    o_ref = refs[2 * n_lhs + 1]
    ab_refs = refs[2 * n_lhs + 2:]

    @pl.when(pl.program_id(1) == 0)
    def _():
        for a_ref, ab_ref in zip(a_refs, ab_refs):
            ab_ref[...] = a_ref[...].astype(MXU_DTYPE)

    acc = x_ref[...]
    for ab_ref, w_ref in zip(ab_refs, w_refs):
        acc = acc + jnp.dot(ab_ref[...], w_ref[...], preferred_element_type=F32)
    o_ref[...] = acc


def _mm_res(lhs, ws, x, *, tm, tn):
    t, d = x.shape
    n_lhs = len(lhs)
    in_specs = [pl.BlockSpec((tm, a.shape[1]), lambda i, j: (i, 0)) for a in lhs]
    in_specs += [pl.BlockSpec((w.shape[0], tn), lambda i, j: (0, j)) for w in ws]
    in_specs += [pl.BlockSpec((tm, tn), lambda i, j: (i, j))]
    return pl.pallas_call(
        functools.partial(_mm_res_kernel, n_lhs=n_lhs), grid=(t // tm, d // tn),
        in_specs=in_specs, out_specs=pl.BlockSpec((tm, tn), lambda i, j: (i, j)),
        out_shape=jax.ShapeDtypeStruct((t, d), F32),
        scratch_shapes=[pltpu.VMEM((tm, a.shape[1]), MXU_DTYPE) for a in lhs],
        compiler_params=_cparams("parallel", "arbitrary"), name="out_proj",
    )(*lhs, *ws, x)


def _swa_kernel(sink_ref, q_ref, kp_ref, kc_ref, vp_ref, vc_ref, o_ref):
    n = pl.program_id(1)
    g_sz = A_Q_HEADS // A_KV_HEADS
    dh = A_HEAD_DIM
    q = q_ref[...]
    k2 = jnp.concatenate([kp_ref[...], kc_ref[...]], axis=0).astype(MXU_DTYPE)
    v2 = jnp.concatenate([vp_ref[...], vc_ref[...]], axis=0).astype(MXU_DTYPE)
    qi = lax.broadcasted_iota(jnp.int32, (BLOCK, 2 * BLOCK), 0)
    kj = lax.broadcasted_iota(jnp.int32, (BLOCK, 2 * BLOCK), 1)
    diff = qi + BLOCK - kj
    valid = (diff >= 0) & (diff < SWA_WINDOW) & ((kj >= BLOCK) | (n > 0))
    scale = dh ** -0.5
    for h in range(A_KV_HEADS):
        kh = k2[:, h * dh:(h + 1) * dh]
        vh = v2[:, h * dh:(h + 1) * dh]
        for g in range(g_sz):
            hq = h * g_sz + g
            s = _dot_nt(q[:, hq * dh:(hq + 1) * dh], kh) * scale
            s = jnp.where(valid, s, NEG_INF)
            sink = sink_ref[hq]
            m = jnp.maximum(jnp.max(s, axis=-1, keepdims=True), sink)
            p = jnp.exp(s - m)
            denom = jnp.sum(p, axis=-1, keepdims=True) + jnp.exp(sink - m)
            o_ref[:, hq * dh:(hq + 1) * dh] = _dot(p, vh) / denom


def _swa(p0, sinks, b, s):
    nb = s // BLOCK
    kv_w = A_KV_COLS
    k_blk = A_Q_COLS // kv_w
    v_blk = k_blk + 1
    cur = lambda bi, n: bi * nb + n
    prev = lambda bi, n: bi * nb + jnp.maximum(n - 1, 0)
    return pl.pallas_call(
        _swa_kernel, grid=(b, nb),
        in_specs=[pl.BlockSpec(memory_space=pltpu.SMEM),
                  pl.BlockSpec((BLOCK, A_Q_COLS), lambda bi, n: (cur(bi, n), 0)),
                  pl.BlockSpec((BLOCK, kv_w), lambda bi, n: (prev(bi, n), k_blk)),
                  pl.BlockSpec((BLOCK, kv_w), lambda bi, n: (cur(bi, n), k_blk)),
                  pl.BlockSpec((BLOCK, kv_w), lambda bi, n: (prev(bi, n), v_blk)),
                  pl.BlockSpec((BLOCK, kv_w), lambda bi, n: (cur(bi, n), v_blk))],
        out_specs=pl.BlockSpec((BLOCK, A_Q_COLS), lambda bi, n: (cur(bi, n), 0)),
        out_shape=jax.ShapeDtypeStruct((b * s, A_Q_COLS), F32),
        compiler_params=_cparams("parallel", "arbitrary"), name="swa",
    )(sinks, p0, p0, p0, p0, p0)


CONV_ROWS = 256
CONV_COLS = 512
HALO = 8


def _conv_kernel(bg_ref, cg_ref, hx_ref, cgh_ref, hxh_ref, w_ref, o_ref, *, tiles_per_seq):
    i = pl.program_id(0)
    z = cg_ref[...] * hx_ref[...]
    first = (i % tiles_per_seq) == 0
    zh = jnp.where(first, 0.0, cgh_ref[...] * hxh_ref[...])
    row = lax.broadcasted_iota(jnp.int32, z.shape, 0)
    z1 = pltpu.roll(z, 1, 0)
    z1 = jnp.where(row == 0, zh[HALO - 1:HALO], z1)
    z2 = pltpu.roll(z, 2, 0)
    z2 = jnp.where(row == 0, zh[HALO - 2:HALO - 1], jnp.where(row == 1, zh[HALO - 1:HALO], z2))
    w = w_ref[...]
    o_ref[...] = bg_ref[...] * (w[0:1] * z2 + w[1:2] * z1 + w[2:3] * z)


def _conv(p0, conv_w, s):
    t = p0.shape[0]
    col0 = A_Q_COLS + 2 * A_KV_COLS
    bg_blk, cg_blk, hx_blk = (col0 // CONV_COLS, (col0 + B_WIDTH) // CONV_COLS,
                              (col0 + 2 * B_WIDTH) // CONV_COLS)
    rh = CONV_ROWS // HALO
    halo_row = lambda i: jnp.maximum(i * rh - 1, 0)
    return pl.pallas_call(
        functools.partial(_conv_kernel, tiles_per_seq=s // CONV_ROWS),
        grid=(t // CONV_ROWS, B_WIDTH // CONV_COLS),
        in_specs=[pl.BlockSpec((CONV_ROWS, CONV_COLS), lambda i, j: (i, bg_blk + j)),
                  pl.BlockSpec((CONV_ROWS, CONV_COLS), lambda i, j: (i, cg_blk + j)),
                  pl.BlockSpec((CONV_ROWS, CONV_COLS), lambda i, j: (i, hx_blk + j)),
                  pl.BlockSpec((HALO, CONV_COLS), lambda i, j: (halo_row(i), cg_blk + j)),
                  pl.BlockSpec((HALO, CONV_COLS), lambda i, j: (halo_row(i), hx_blk + j)),
                  pl.BlockSpec((3, CONV_COLS), lambda i, j: (0, j))],
        out_specs=pl.BlockSpec((CONV_ROWS, CONV_COLS), lambda i, j: (i, j)),
        out_shape=jax.ShapeDtypeStruct((t, B_WIDTH), F32),
        compiler_params=_cparams("parallel", "parallel"), name="gated_conv",
    )(p0, p0, p0, p0, p0, conv_w)


def _dsa_kernel(qa_ref, qb_ref, iq_ref, iwq_ref, k_ref, v_ref, ikw_ref, o_ref, key_ref, *, k_sel):
    n = pl.program_id(1)
    s_len = k_ref.shape[0]
    ik = ikw_ref[:, 0:IDX_HEAD_DIM].astype(MXU_DTYPE)
    iw = iwq_ref[:, IDX_HEAD_DIM:IDX_HEAD_DIM + IDX_HEADS] * (
        (IDX_HEADS ** -0.5) * (IDX_HEAD_DIM ** -0.5))
    iq = iq_ref[...]
    score = jnp.zeros((BLOCK, s_len), F32)
    for h in range(IDX_HEADS):
        lg = _dot_nt(iq[:, h * IDX_HEAD_DIM:(h + 1) * IDX_HEAD_DIM], ik)
        score = score + iw[:, h:h + 1] * jnp.maximum(lg, 0.0)
    score = score + 0.0
    qpos = n * BLOCK + lax.broadcasted_iota(jnp.int32, (BLOCK, s_len), 0)
    kpos = lax.broadcasted_iota(jnp.int32, (BLOCK, s_len), 1)
    causal = kpos <= qpos
    bits = pltpu.bitcast(score, jnp.int32)
    key = jnp.where(bits < 0, bits ^ jnp.int32(0x7FFFFFFF), bits)
    key_ref[...] = jnp.where(causal, key, INT_MIN)

    def count_ge(thr):
        return jnp.sum(jnp.where(key_ref[...] >= thr, 1.0, 0.0), axis=-1, keepdims=True)

    kf = float(k_sel)
    thr0 = jnp.where(count_ge(jnp.zeros((BLOCK, 1), jnp.int32)) >= kf,
                     jnp.int32(0), jnp.int32(INT_MIN))

    def bit_step(it, thr):
        cand = thr + jnp.left_shift(jnp.int32(1), jnp.int32(30) - it)
        return jnp.where(count_ge(cand) >= kf, cand, thr)

    thr = lax.fori_loop(0, 31, bit_step, thr0)
    thr = jnp.maximum(thr, INT_MIN + 1)
    sel = key_ref[...] >= thr

    kk = k_ref[...].astype(MXU_DTYPE)
    vv = v_ref[...].astype(MXU_DTYPE)
    scale = C_HEAD_DIM ** -0.5
    for h in range(C_HEADS):
        q_r = qa_ref if h < C_HEADS // 2 else qb_ref
        hh = h % (C_HEADS // 2)
        s = _dot_nt(q_r[:, hh * C_HEAD_DIM:(hh + 1) * C_HEAD_DIM], kk) * scale
        s = jnp.where(sel, s, NEG_INF)
        m = jnp.max(s, axis=-1, keepdims=True)
        p = jnp.exp(s - m)
        denom = jnp.sum(p, axis=-1, keepdims=True)
        o_ref[:, h * C_HEAD_DIM:(h + 1) * C_HEAD_DIM] = _dot(p, vv) / denom


def _dsa(u, p1, b, s):
    nb = s // BLOCK
    k_sel = min(INDEX_TOPK, s // 4)
    half_q = C_HEADS * C_HEAD_DIM // 2
    k_blk = C_Q_RANK // LANES
    row = lambda bi, n: bi * nb + n
    return pl.pallas_call(
        functools.partial(_dsa_kernel, k_sel=k_sel), grid=(b, nb),
        in_specs=[pl.BlockSpec((BLOCK, half_q), lambda bi, n: (row(bi, n), 0)),
                  pl.BlockSpec((BLOCK, half_q), lambda bi, n: (row(bi, n), 1)),
                  pl.BlockSpec((BLOCK, half_q), lambda bi, n: (row(bi, n), 2)),
                  pl.BlockSpec((BLOCK, LANES), lambda bi, n: (row(bi, n), k_blk + 2)),
                  pl.BlockSpec((s, LANES), lambda bi, n: (bi, k_blk)),
                  pl.BlockSpec((s, LANES), lambda bi, n: (bi, k_blk + 1)),
                  pl.BlockSpec((s, LANES), lambda bi, n: (bi, k_blk + 2))],
        out_specs=pl.BlockSpec((BLOCK, C_HEADS * C_HEAD_DIM), lambda bi, n: (row(bi, n), 0)),
        out_shape=jax.ShapeDtypeStruct((b * s, C_HEADS * C_HEAD_DIM), F32),
        scratch_shapes=[pltpu.VMEM((BLOCK, s), jnp.int32)],
        compiler_params=_cparams("parallel", "arbitrary"), name="dsa",
    )(u, u, u, p1, p1, p1, p1)


ROUTER_TOKENS = 256
N_FULL_K0 = 8
SUB = 8


def _top_rows(work, k, exact):
    rows, cols = work.shape
    row = lax.broadcasted_iota(jnp.int32, (rows, cols), 0).astype(F32)
    vrow = lax.broadcasted_iota(jnp.int32, (k, cols), 0)
    rank = jnp.full((rows, cols), float(k), F32)
    vals = jnp.zeros((k, cols), F32)
    for r in range(k):
        m = jnp.max(work, axis=0, keepdims=True)
        hit = work == m
        if exact:
            first = jnp.min(jnp.where(hit, row, float(rows)), axis=0, keepdims=True)
            hit = row == first
        rank = jnp.where(hit, float(r), rank)
        work = jnp.where(hit, -jnp.inf, work)
        vals = jnp.where(vrow == r, m, vals)
    return rank, vals


def _count_ranked(rank, k):
    return jnp.sum(jnp.where(rank < float(k), 1.0, 0.0), axis=0, keepdims=True)


def _pair_counts_exact(v0, v1, k):
    cand = jnp.concatenate([v0[k0:k0 + 1] + v1 for k0 in range(N_FULL_K0)]
                           + [v0[N_FULL_K0:] + v1[0:1]], axis=0)
    rank_c, _ = _top_rows(cand, k, True)
    sel = rank_c < float(k)
    cnt = jnp.where(sel, 1.0, 0.0)
    n_rank = jnp.concatenate(
        [jnp.sum(cnt[k0 * k:(k0 + 1) * k], axis=0, keepdims=True) for k0 in range(N_FULL_K0)]
        + [cnt[N_FULL_K0 * k:]], axis=0)
    z = jnp.sum(jnp.where(sel, jnp.exp(cand - cand[0:1]), 0.0), axis=0, keepdims=True)
    return n_rank, z, jnp.sum(cnt, axis=0, keepdims=True)


def _pair_counts_fast(v0, v1, k):
    cols = v0.shape[1]
    row = lax.broadcasted_iota(jnp.int32, (SUB, cols), 0)
    tiles = []
    for k1 in range(SUB):
        n_valid = min(k // (k1 + 1), SUB)
        tiles.append(jnp.where(row < n_valid, v0[0:SUB] + v1[k1:k1 + 1], -jnp.inf))
    tiles.append(v0[0:1] + v1[SUB:])
    tiles.append(v0[SUB:] + v1[0:1])
    cand = jnp.concatenate(tiles, axis=0)
    rank_c, _ = _top_rows(cand, k, False)
    sel = rank_c < float(k)
    cnt = jnp.where(sel, 1.0, 0.0)
    low = cnt[0:SUB]
    for k1 in range(1, SUB):
        low = low + cnt[k1 * SUB:(k1 + 1) * SUB]
    tail = jnp.sum(cnt[SUB * SUB:SUB * SUB + SUB], axis=0, keepdims=True)
    low = low + jnp.where(row == 0, tail, 0.0)
    n_rank = jnp.concatenate([low, cnt[SUB * SUB + SUB:]], axis=0)
    z = jnp.sum(jnp.where(sel, jnp.exp(cand - (v0[0:1] + v1[0:1])), 0.0), axis=0, keepdims=True)
    return n_rank, z, jnp.sum(cnt, axis=0, keepdims=True)


def _router_kernel(q_ref, keys_ref, p0_ref, n0_ref, p1_ref, r1_ref):
    k = PEER_TOPK

    def head(h, carry):
        col0 = pl.multiple_of(h * (2 * PEER_HALF), LANES)
        col1 = pl.multiple_of(col0 + PEER_HALF, LANES)
        s0 = _dot_nt(keys_ref[2 * h], q_ref[:, pl.ds(col0, PEER_HALF)])
        s1 = _dot_nt(keys_ref[2 * h + 1], q_ref[:, pl.ds(col1, PEER_HALF)])

        def route(exact):
            rank0, v0 = _top_rows(s0, k, exact)
            rank1, v1 = _top_rows(s1, k, exact)
            n_rank, z, n_sel = (_pair_counts_exact if exact else _pair_counts_fast)(v0, v1, k)
            n0 = jnp.zeros_like(s0)
            for r in range(k):
                n0 = jnp.where(rank0 == float(r), n_rank[r:r + 1], n0)
            p0_ref[h] = jnp.where(rank0 < float(k), jnp.exp(s0 - v0[0:1]), 0.0) / z
            n0_ref[h] = n0
            p1_ref[h] = jnp.where(rank1 < float(k), jnp.exp(s1 - v1[0:1]), 0.0).astype(p1_ref.dtype)
            r1_ref[h] = rank1.astype(r1_ref.dtype)
            return _count_ranked(rank0, k) + _count_ranked(rank1, k) + n_sel

        n_marked = route(False)

        @pl.when(jnp.max(n_marked) > 3.0 * k)
        def _():
            route(True)

        return carry

    lax.fori_loop(0, PEER_HEADS, head, 0)


def _router(q, keys):
    t = q.shape[0]
    tb = ROUTER_TOKENS
    blk = pl.BlockSpec((PEER_HEADS, PEER_N_KEYS, tb), lambda i: (0, 0, i))
    f32s = jax.ShapeDtypeStruct((PEER_HEADS, PEER_N_KEYS, t), F32)
    bf16s = jax.ShapeDtypeStruct((PEER_HEADS, PEER_N_KEYS, t), BF16)
    return pl.pallas_call(
        _router_kernel, grid=(t // tb,),
        in_specs=[pl.BlockSpec((tb, q.shape[1]), lambda i: (i, 0)),
                  pl.BlockSpec(keys.shape, lambda i: (0, 0, 0))],
        out_specs=[blk, blk, blk, blk], out_shape=[f32s, f32s, bf16s, bf16s],
        compiler_params=_cparams("parallel"), name="peer_router",
    )(q, keys)


EXPERT_TOKENS = 512
EXPERT_CHUNK = 1024
ROWS_PER_CHUNK = EXPERT_CHUNK // PEER_N_KEYS


PACK = 16


STEP_PARTS = 4


def _gated_hidden(pre_ref, p0_ref, n0_ref, p1_ref, r1_ref, h_ref, part):
    tb = pre_ref.shape[1]
    per = ROWS_PER_CHUNK // STEP_PARTS
    for ii in range(part * per, (part + 1) * per):
        gates = [jnp.broadcast_to(p0_ref[h, ii:ii + 1, :], (PACK, tb)).astype(BF16)
                 for h in range(PEER_HEADS)]
        cnts = [jnp.broadcast_to(n0_ref[h, ii:ii + 1, :], (PACK, tb)).astype(BF16)
                for h in range(PEER_HEADS)]
        for rt in range(PEER_N_KEYS // PACK):
            js = slice(rt * PACK, (rt + 1) * PACK)
            w = None
            for h in range(PEER_HEADS):
                term = jnp.where(r1_ref[h, js, :] < cnts[h], p1_ref[h, js, :],
                                 jnp.zeros((), BF16)) * gates[h]
                w = term if w is None else w + term
            rows = slice(ii * PEER_N_KEYS + rt * PACK, ii * PEER_N_KEYS + (rt + 1) * PACK)
            pre = pre_ref[rows, :]
            act = 0.5 * pre * (1.0 + lax.erf(pre * np.sqrt(0.5).astype(np.float32)))
            h_ref[rows, :] = act.astype(BF16) * w


def _expert_kernel(xnt_ref, x_ref, u_ref, vt_ref, p0_ref, n0_ref, p1_ref, r1_ref, o_ref, acc_ref,
                   h0_ref, h1_ref, pre0_ref, pre1_ref, *, nc):
    g = pl.program_id(0)

    @pl.when(g == 0)
    def _():
        pre1_ref[...] = jnp.zeros_like(pre1_ref)
        h0_ref[...] = jnp.zeros_like(h0_ref)

    @pl.when((g == 0) | (g % nc == 2))
    def _():
        acc_ref[...] = jnp.zeros_like(acc_ref)

    def step(pre_new, pre_old, h_new, h_old):
        ec, d = u_ref.shape
        for part in range(STEP_PARTS):
            ru = slice(part * ec // STEP_PARTS, (part + 1) * ec // STEP_PARTS)
            rv = slice(part * d // STEP_PARTS, (part + 1) * d // STEP_PARTS)
            pre_new[ru, :] = jnp.dot(u_ref[ru, :], xnt_ref[...], preferred_element_type=F32)
            acc_ref[rv, :] += jnp.dot(vt_ref[rv, :], h_old[...], preferred_element_type=F32)
            _gated_hidden(pre_old, p0_ref, n0_ref, p1_ref, r1_ref, h_new, part)

    @pl.when(g % 2 == 0)
    def _():
        step(pre0_ref, pre1_ref, h1_ref, h0_ref)

    @pl.when(g % 2 == 1)
    def _():
        step(pre1_ref, pre0_ref, h0_ref, h1_ref)

    @pl.when((g > 1) & (g % nc == 1))
    def _():
        o_ref[...] = x_ref[...] + acc_ref[...].T


def _experts(xnt, x, u, vt, p0, n0, p1, r1):
    t, d = x.shape
    tb, ec = EXPERT_TOKENS, EXPERT_CHUNK
    nc = PEER_EXPERTS // ec
    n_pairs = (t // tb) * nc
    assert nc > 2
    first = lambda g: jnp.minimum(g, n_pairs - 1)
    gate = lambda g: jnp.clip(g - 1, 0, n_pairs - 1)
    last = lambda g: jnp.maximum(g - 2, 0)
    once = pl.Buffered(1)
    small = pl.BlockSpec((PEER_HEADS, ROWS_PER_CHUNK, tb), lambda g: (0, gate(g) % nc, gate(g) // nc))
    full = pl.BlockSpec((PEER_HEADS, PEER_N_KEYS, tb), lambda g: (0, 0, gate(g) // nc))
    return pl.pallas_call(
        functools.partial(_expert_kernel, nc=nc), grid=(n_pairs + 2,),
        in_specs=[pl.BlockSpec((d, tb), lambda g: (0, first(g) // nc)),
                  pl.BlockSpec((tb, d), lambda g: (last(g) // nc, 0), pipeline_mode=once),
                  pl.BlockSpec((ec, d), lambda g: (first(g) % nc, 0)),
                  pl.BlockSpec((d, ec), lambda g: (0, last(g) % nc)),
                  small, small, full, full],
        out_specs=pl.BlockSpec((tb, d), lambda g: (last(g) // nc, 0), pipeline_mode=once),
        out_shape=jax.ShapeDtypeStruct((t, d), F32),
        scratch_shapes=[pltpu.VMEM((d, tb), F32), pltpu.VMEM((ec, tb), BF16),
                        pltpu.VMEM((ec, tb), BF16), pltpu.VMEM((ec, tb), F32),
                        pltpu.VMEM((ec, tb), F32)],
        compiler_params=_cparams("arbitrary"), name="peer_experts",
    )(xnt, x, u, vt, p0, n0, p1, r1)


def _peer(x, g, wq, keys, u, v):
    xq, xn = _proj(x, g, wq.astype(MXU_DTYPE), k_cols=x.shape[1], tm=512, tn=512, with_xn=True)
    keys2 = keys.reshape(2 * PEER_HEADS, PEER_N_KEYS, PEER_HALF).astype(MXU_DTYPE)
    p0, n0, p1, r1 = _router(xq, keys2)
    return _experts(xn.T, x, u.astype(MXU_DTYPE), v.astype(MXU_DTYPE).T, p0, n0, p1, r1)


def _norm_kernel(x_ref, g_ref, o_ref):
    x = x_ref[...]
    ms = jnp.mean(x * x, axis=-1, keepdims=True)
    o_ref[...] = x * lax.rsqrt(ms + EPS) * g_ref[...]


def _norm(x, g, tm=512):
    t, d = x.shape
    return pl.pallas_call(
        _norm_kernel, grid=(t // tm,),
        in_specs=[pl.BlockSpec((tm, d), lambda i: (i, 0)), pl.BlockSpec((1, d), lambda i: (0, 0))],
        out_specs=pl.BlockSpec((tm, d), lambda i: (i, 0)),
        out_shape=jax.ShapeDtypeStruct((t, d), F32),
        compiler_params=_cparams("parallel"), name="final_norm",
    )(x, g.reshape(1, d))


def kernel(x, positions, norm_mix0, w_in0, sinks0, conv_w0, w_out0, norm_ffn0, peer_wq0, peer_keys0,
           peer_u0, peer_v0, norm_mix1, w_in1, g_qa1, w_uq1, w_out1, norm_ffn1, peer_wq1,
           peer_keys1, peer_u1, peer_v1, norm_final):
    b, s, d = x.shape
    t = b * s
    xf = x.reshape(t, d)
    pos = positions.reshape(t)
    tab_a = _rope_table(pos, ((0, 64), (64, 64)), 8)
    tab_b = _rope_table(pos, ((0, 128),), 16)
    tab_c = _rope_table(pos, ((0, 64),), 8)

    tn0 = 256
    n_rope = (A_Q_COLS + A_KV_COLS) // tn0
    groups0 = (((0, n_rope), (0, 0)), ((n_rope, w_in0.shape[1] // tn0), (-1, -1)))
    p0 = _proj(xf, norm_mix0, w_in0.astype(MXU_DTYPE), k_cols=d, tm=512, tn=tn0,
               tabs=(tab_a,), halves=(8,), groups=groups0)
    a_out = _swa(p0, sinks0, b, s)
    b_out = _conv(p0, conv_w0, s)
    w_o = w_out0.astype(MXU_DTYPE)
    x1 = _mm_res([a_out, b_out], [w_o[:A_Q_COLS], w_o[A_Q_COLS:]], xf, tm=512, tn=512)
    x2 = _peer(x1, norm_ffn0, peer_wq0, peer_keys0, peer_u0, peer_v0)

    w1 = jnp.pad(w_in1, ((0, 0), (0, IN1_COLS_PADDED - w_in1.shape[1]))).astype(MXU_DTYPE)
    k_chunk = C_Q_RANK // LANES
    cfg1 = tuple({k_chunk: 0, k_chunk + 2: 1}.get(c, -1) for c in range(IN1_COLS_PADDED // LANES))
    p1 = _proj(x2, norm_mix1, w1, k_cols=d, tm=512, tn=IN1_COLS_PADDED,
               tabs=(tab_b, tab_c), halves=(16, 8), groups=(((0, 1), cfg1),))
    n_q = C_HEADS * C_HEAD_DIM // LANES
    n_iq = IDX_HEADS * IDX_HEAD_DIM // LANES
    cfg_u = (0,) * n_q + (1,) * n_iq
    u = _proj(p1, g_qa1, w_uq1.astype(MXU_DTYPE), k_cols=C_Q_RANK, tm=256,
              tn=(n_q + n_iq) * LANES, tabs=(tab_b, tab_a), halves=(16, 8),
              groups=(((0, 1), cfg_u),))
    o = _dsa(u, p1, b, s)
    x3 = _mm_res([o], [w_out1.astype(MXU_DTYPE)], x2, tm=512, tn=512)
    x4 = _peer(x3, norm_ffn1, peer_wq1, peer_keys1, peer_u1, peer_v1)
    return _norm(x4, norm_final).reshape(b, s, d)
```

```python
import functools

import numpy as np
import jax
import jax.numpy as jnp
from jax import lax
from jax.experimental import pallas as pl
from jax.experimental.pallas import tpu as pltpu

F32 = jnp.float32
BF16 = jnp.bfloat16
MXU_DTYPE = jnp.bfloat16

LANES = 128
EPS = 1e-5
ROPE_THETA = 500000.0
BLOCK = 128
NEG_INF = -1e30
INT_MIN = -(2 ** 31)

SWA_WINDOW = 128
A_Q_HEADS, A_KV_HEADS, A_HEAD_DIM = 16, 4, 64
A_Q_COLS = A_Q_HEADS * A_HEAD_DIM
A_KV_COLS = A_KV_HEADS * A_HEAD_DIM
B_WIDTH = 1024
C_Q_RANK, C_HEADS, C_HEAD_DIM = 512, 16, 128
IDX_HEADS, IDX_HEAD_DIM, INDEX_TOPK = 16, 64, 256
IN1_COLS_PADDED = 896
PEER_HEADS, PEER_N_KEYS, PEER_TOPK, PEER_HALF = 8, 128, 16, 128
PEER_EXPERTS = PEER_N_KEYS * PEER_N_KEYS

VMEM_LIMIT = 56 * 1024 * 1024


def _cparams(*sem):
    return pltpu.CompilerParams(dimension_semantics=sem, vmem_limit_bytes=VMEM_LIMIT)


def _dot(a, b):
    return jnp.dot(a.astype(MXU_DTYPE), b.astype(MXU_DTYPE), preferred_element_type=F32)


def _dot_nt(a, b):
    return lax.dot_general(a.astype(MXU_DTYPE), b.astype(MXU_DTYPE),
                           (((1,), (1,)), ((), ())), preferred_element_type=F32)


def _rope_table(pos, heads, half):
    rot = 2 * half
    inv = ROPE_THETA ** (-jnp.arange(half, dtype=F32) * 2.0 / rot)
    freq = np.zeros(LANES, np.int32)
    rotated = np.zeros(LANES, bool)
    upper = np.zeros(LANES, bool)
    for lane0, dh in heads:
        assert dh // 8 == half
        freq[lane0:lane0 + rot] = np.arange(rot) % half
        rotated[lane0:lane0 + rot] = True
        upper[lane0 + half:lane0 + rot] = True
    inv_lane = jnp.where(rotated, inv[freq], 0.0)
    ang = pos.astype(F32)[:, None] * inv_lane[None, :]
    cos, sin = jnp.cos(ang), jnp.sin(ang)
    s1 = jnp.where(upper, sin, 0.0)
    s2 = jnp.where(rotated & ~upper, -sin, 0.0)
    return jnp.stack([cos, s1, s2])


def _rope_chunk(xc, tab_ref, half):
    return (xc * tab_ref[0] + pltpu.roll(xc, half, 1) * tab_ref[1]
            + pltpu.roll(xc, LANES - half, 1) * tab_ref[2])


def _proj_kernel(*refs, n_tabs, halves, groups, tn, with_xn):
    x_ref, g_ref, w_ref = refs[:3]
    tab_refs = refs[3:3 + n_tabs]
    o_ref = refs[3 + n_tabs]
    xn_out = refs[4 + n_tabs] if with_xn else None
    xn_ref = refs[-1]
    j = pl.program_id(1)

    @pl.when(j == 0)
    def _():
        x = x_ref[...]
        ms = jnp.mean(x * x, axis=-1, keepdims=True)
        xn = (x * lax.rsqrt(ms + EPS) * g_ref[...]).astype(MXU_DTYPE)
        xn_ref[...] = xn
        if with_xn:
            xn_out[...] = xn

    acc = jnp.dot(xn_ref[...], w_ref[...], preferred_element_type=F32)

    for (j_lo, j_hi), cfg in groups:
        @pl.when((j >= j_lo) & (j < j_hi))
        def _(cfg=cfg):
            for c in range(tn // LANES):
                xc = acc[:, c * LANES:(c + 1) * LANES]
                if cfg[c] >= 0:
                    xc = _rope_chunk(xc, tab_refs[cfg[c]], halves[cfg[c]])
                o_ref[:, c * LANES:(c + 1) * LANES] = xc


def _proj(x, g, w, *, k_cols, tm, tn, tabs=(), halves=(), groups=None, with_xn=False):
    t = x.shape[0]
    n = w.shape[1]
    nj = n // tn
    if groups is None:
        groups = (((0, nj), (-1,) * (tn // LANES)),)
    kern = functools.partial(_proj_kernel, n_tabs=len(tabs), halves=tuple(halves),
                             groups=tuple(groups), tn=tn, with_xn=with_xn)
    in_specs = [pl.BlockSpec((tm, k_cols), lambda i, j: (i, 0)),
                pl.BlockSpec((1, k_cols), lambda i, j: (0, 0)),
                pl.BlockSpec((k_cols, tn), lambda i, j: (0, j))]
    in_specs += [pl.BlockSpec((3, tm, LANES), lambda i, j: (0, i, 0)) for _ in tabs]
    out_shape = [jax.ShapeDtypeStruct((t, n), F32)]
    out_specs = [pl.BlockSpec((tm, tn), lambda i, j: (i, j))]
    if with_xn:
        out_shape.append(jax.ShapeDtypeStruct((t, k_cols), MXU_DTYPE))
        out_specs.append(pl.BlockSpec((tm, k_cols), lambda i, j: (i, 0)))
    res = pl.pallas_call(
        kern, grid=(t // tm, nj), in_specs=in_specs, out_specs=out_specs, out_shape=out_shape,
        scratch_shapes=[pltpu.VMEM((tm, k_cols), MXU_DTYPE)],
        compiler_params=_cparams("parallel", "arbitrary"), name="norm_proj",
    )(x, g.reshape(1, k_cols), w, *tabs)
    return res if with_xn else res[0]


def _mm_res_kernel(*refs, n_lhs):
    a_refs = refs[:n_lhs]
    w_refs = refs[n_lhs:2 * n_lhs]
    x_ref = refs[2 * n_lhs]
    o_ref = refs[2 * n_lhs + 1]
    ab_refs = refs[2 * n_lhs + 2:]

    @pl.when(pl.program_id(1) == 0)
    def _():
        for a_ref, ab_ref in zip(a_refs, ab_refs):
            ab_ref[...] = a_ref[...].astype(MXU_DTYPE)

    acc = x_ref[...]
    for ab_ref, w_ref in zip(ab_refs, w_refs):
        acc = acc + jnp.dot(ab_ref[...], w_ref[...], preferred_element_type=F32)
    o_ref[...] = acc


def _mm_res(lhs, ws, x, *, tm, tn):
    t, d = x.shape
    n_lhs = len(lhs)
    in_specs = [pl.BlockSpec((tm, a.shape[1]), lambda i, j: (i, 0)) for a in lhs]
    in_specs += [pl.BlockSpec((w.shape[0], tn), lambda i, j: (0, j)) for w in ws]
    in_specs += [pl.BlockSpec((tm, tn), lambda i, j: (i, j))]
    return pl.pallas_call(
        functools.partial(_mm_res_kernel, n_lhs=n_lhs), grid=(t // tm, d // tn),
        in_specs=in_specs, out_specs=pl.BlockSpec((tm, tn), lambda i, j: (i, j)),
        out_shape=jax.ShapeDtypeStruct((t, d), F32),
        scratch_shapes=[pltpu.VMEM((tm, a.shape[1]), MXU_DTYPE) for a in lhs],
        compiler_params=_cparams("parallel", "arbitrary"), name="out_proj",
    )(*lhs, *ws, x)


def _swa_kernel(sink_ref, q_ref, kp_ref, kc_ref, vp_ref, vc_ref, o_ref):
    n = pl.program_id(1)
    g_sz = A_Q_HEADS // A_KV_HEADS
    dh = A_HEAD_DIM
    q = q_ref[...]
    k2 = jnp.concatenate([kp_ref[...], kc_ref[...]], axis=0).astype(MXU_DTYPE)
    v2 = jnp.concatenate([vp_ref[...], vc_ref[...]], axis=0).astype(MXU_DTYPE)
    qi = lax.broadcasted_iota(jnp.int32, (BLOCK, 2 * BLOCK), 0)
    kj = lax.broadcasted_iota(jnp.int32, (BLOCK, 2 * BLOCK), 1)
    diff = qi + BLOCK - kj
    valid = (diff >= 0) & (diff < SWA_WINDOW) & ((kj >= BLOCK) | (n > 0))
    scale = dh ** -0.5
    for h in range(A_KV_HEADS):
        kh = k2[:, h * dh:(h + 1) * dh]
        vh = v2[:, h * dh:(h + 1) * dh]
        for g in range(g_sz):
            hq = h * g_sz + g
            s = _dot_nt(q[:, hq * dh:(hq + 1) * dh], kh) * scale
            s = jnp.where(valid, s, NEG_INF)
            sink = sink_ref[hq]
            m = jnp.maximum(jnp.max(s, axis=-1, keepdims=True), sink)
            p = jnp.exp(s - m)
            denom = jnp.sum(p, axis=-1, keepdims=True) + jnp.exp(sink - m)
            o_ref[:, hq * dh:(hq + 1) * dh] = _dot(p, vh) / denom


def _swa(p0, sinks, b, s):
    nb = s // BLOCK
    kv_w = A_KV_COLS
    k_blk = A_Q_COLS // kv_w
    v_blk = k_blk + 1
    cur = lambda bi, n: bi * nb + n
    prev = lambda bi, n: bi * nb + jnp.maximum(n - 1, 0)
    return pl.pallas_call(
        _swa_kernel, grid=(b, nb),
        in_specs=[pl.BlockSpec(memory_space=pltpu.SMEM),
                  pl.BlockSpec((BLOCK, A_Q_COLS), lambda bi, n: (cur(bi, n), 0)),
                  pl.BlockSpec((BLOCK, kv_w), lambda bi, n: (prev(bi, n), k_blk)),
                  pl.BlockSpec((BLOCK, kv_w), lambda bi, n: (cur(bi, n), k_blk)),
                  pl.BlockSpec((BLOCK, kv_w), lambda bi, n: (prev(bi, n), v_blk)),
                  pl.BlockSpec((BLOCK, kv_w), lambda bi, n: (cur(bi, n), v_blk))],
        out_specs=pl.BlockSpec((BLOCK, A_Q_COLS), lambda bi, n: (cur(bi, n), 0)),
        out_shape=jax.ShapeDtypeStruct((b * s, A_Q_COLS), F32),
        compiler_params=_cparams("parallel", "arbitrary"), name="swa",
    )(sinks, p0, p0, p0, p0, p0)


CONV_ROWS = 256
CONV_COLS = 512
HALO = 8


def _conv_kernel(bg_ref, cg_ref, hx_ref, cgh_ref, hxh_ref, w_ref, o_ref, *, tiles_per_seq):
    i = pl.program_id(0)
    z = cg_ref[...] * hx_ref[...]
    first = (i % tiles_per_seq) == 0
    zh = jnp.where(first, 0.0, cgh_ref[...] * hxh_ref[...])
    row = lax.broadcasted_iota(jnp.int32, z.shape, 0)
    z1 = pltpu.roll(z, 1, 0)
    z1 = jnp.where(row == 0, zh[HALO - 1:HALO], z1)
    z2 = pltpu.roll(z, 2, 0)
    z2 = jnp.where(row == 0, zh[HALO - 2:HALO - 1], jnp.where(row == 1, zh[HALO - 1:HALO], z2))
    w = w_ref[...]
    o_ref[...] = bg_ref[...] * (w[0:1] * z2 + w[1:2] * z1 + w[2:3] * z)


def _conv(p0, conv_w, s):
    t = p0.shape[0]
    col0 = A_Q_COLS + 2 * A_KV_COLS
    bg_blk, cg_blk, hx_blk = (col0 // CONV_COLS, (col0 + B_WIDTH) // CONV_COLS,
                              (col0 + 2 * B_WIDTH) // CONV_COLS)
    rh = CONV_ROWS // HALO
    halo_row = lambda i: jnp.maximum(i * rh - 1, 0)
    return pl.pallas_call(
        functools.partial(_conv_kernel, tiles_per_seq=s // CONV_ROWS),
        grid=(t // CONV_ROWS, B_WIDTH // CONV_COLS),
        in_specs=[pl.BlockSpec((CONV_ROWS, CONV_COLS), lambda i, j: (i, bg_blk + j)),
                  pl.BlockSpec((CONV_ROWS, CONV_COLS), lambda i, j: (i, cg_blk + j)),
                  pl.BlockSpec((CONV_ROWS, CONV_COLS), lambda i, j: (i, hx_blk + j)),
                  pl.BlockSpec((HALO, CONV_COLS), lambda i, j: (halo_row(i), cg_blk + j)),
                  pl.BlockSpec((HALO, CONV_COLS), lambda i, j: (halo_row(i), hx_blk + j)),
                  pl.BlockSpec((3, CONV_COLS), lambda i, j: (0, j))],
        out_specs=pl.BlockSpec((CONV_ROWS, CONV_COLS), lambda i, j: (i, j)),
        out_shape=jax.ShapeDtypeStruct((t, B_WIDTH), F32),
        compiler_params=_cparams("parallel", "parallel"), name="gated_conv",
    )(p0, p0, p0, p0, p0, conv_w)


DSA_WIDTHS = 4


def _dsa_block(qa_ref, qb_ref, iq_ref, iwq_ref, k_ref, v_ref, ikw_ref, o_ref, key_ref, half_ref,
               n, w, k_sel):
    ik = ikw_ref[0:w, 0:IDX_HEAD_DIM].astype(MXU_DTYPE)
    iw = iwq_ref[:, IDX_HEAD_DIM:IDX_HEAD_DIM + IDX_HEADS] * (
        (IDX_HEADS ** -0.5) * (IDX_HEAD_DIM ** -0.5))
    iq = iq_ref[...]
    score = jnp.zeros((BLOCK, w), F32)
    for h in range(IDX_HEADS):
        lg = _dot_nt(iq[:, h * IDX_HEAD_DIM:(h + 1) * IDX_HEAD_DIM], ik)
        score = score + iw[:, h:h + 1] * jnp.maximum(lg, 0.0)
    score = score + 0.0
    qpos = n * BLOCK + lax.broadcasted_iota(jnp.int32, (BLOCK, w), 0)
    kpos = lax.broadcasted_iota(jnp.int32, (BLOCK, w), 1)
    bits = pltpu.bitcast(score, jnp.int32)
    key = jnp.where(bits < 0, bits ^ jnp.int32(0x7FFFFFFF), bits)
    key = jnp.where(kpos <= qpos, key, INT_MIN)
    key_ref[:, 0:w] = key
    half_ref[:, 0:w] = jnp.right_shift(key, 16).astype(jnp.int16)

    def search16():
        def count_ge(t):
            t16 = t.astype(jnp.int16)
            acc = None
            for c in range(w // LANES):
                one = jnp.where(half_ref[:, c * LANES:(c + 1) * LANES] >= t16,
                                jnp.int16(1), jnp.int16(0))
                acc = one if acc is None else acc + one
            return jnp.sum(acc.astype(F32), axis=-1, keepdims=True)

        kf = float(k_sel)
        t0 = jnp.where(count_ge(jnp.zeros((BLOCK, 1), jnp.int32)) >= kf,
                       jnp.int32(0), jnp.int32(-2 ** 15))

        def bit_step(it, t):
            cand = t + jnp.left_shift(jnp.int32(1), jnp.int32(14) - it)
            return jnp.where(count_ge(cand) >= kf, cand, t)

        return lax.fori_loop(0, 15, bit_step, t0)

    t_hi = search16()
    hi = half_ref[:, 0:w].astype(jnp.int32)
    lo = jnp.bitwise_and(key_ref[:, 0:w], 0xFFFF) - 2 ** 15
    half_ref[:, 0:w] = jnp.where(hi > t_hi, 2 ** 15 - 1,
                                 jnp.where(hi < t_hi, -2 ** 15, lo)).astype(jnp.int16)
    t_lo = search16()
    thr = jnp.maximum(t_hi * 2 ** 16 + (t_lo + 2 ** 15), INT_MIN + 1)
    sel = key_ref[:, 0:w] >= thr

    kk = k_ref[0:w, :].astype(MXU_DTYPE)
    lane = lax.broadcasted_iota(jnp.int32, (w, LANES), 1)
    v_aug = jnp.concatenate([v_ref[0:w, :].astype(MXU_DTYPE),
                             jnp.where(lane == 0, 1.0, 0.0).astype(MXU_DTYPE)], axis=1)
    scale = C_HEAD_DIM ** -0.5
    for h in range(C_HEADS):
        q_r = qa_ref if h < C_HEADS // 2 else qb_ref
        hh = h % (C_HEADS // 2)
        s = _dot_nt(q_r[:, hh * C_HEAD_DIM:(hh + 1) * C_HEAD_DIM] * scale, kk)
        s = jnp.where(sel, s, NEG_INF)
        p = jnp.exp(s - jnp.max(s, axis=-1, keepdims=True))
        pv = _dot(p, v_aug)
        o_ref[:, h * C_HEAD_DIM:(h + 1) * C_HEAD_DIM] = (
            pv[:, :C_HEAD_DIM] / pv[:, C_HEAD_DIM:C_HEAD_DIM + 1])


def _dsa_kernel(qa_ref, qb_ref, iq_ref, iwq_ref, k_ref, v_ref, ikw_ref, o_ref, key_ref, half_ref,
                *, k_sel):
    n = pl.program_id(1)
    chunk = k_ref.shape[0] // DSA_WIDTHS
    n_chunks = (n * BLOCK + BLOCK - 1) // chunk + 1
    for wi in range(1, DSA_WIDTHS + 1):
        @pl.when(n_chunks == wi)
        def _(wi=wi):
            _dsa_block(qa_ref, qb_ref, iq_ref, iwq_ref, k_ref, v_ref, ikw_ref, o_ref, key_ref,
                       half_ref, n, wi * chunk, k_sel)


def _dsa(u, p1, b, s):
    nb = s // BLOCK
    k_sel = min(INDEX_TOPK, s // 4)
    half_q = C_HEADS * C_HEAD_DIM // 2
    k_blk = C_Q_RANK // LANES
    row = lambda bi, n: bi * nb + n
    return pl.pallas_call(
        functools.partial(_dsa_kernel, k_sel=k_sel), grid=(b, nb),
        in_specs=[pl.BlockSpec((BLOCK, half_q), lambda bi, n: (row(bi, n), 0)),
                  pl.BlockSpec((BLOCK, half_q), lambda bi, n: (row(bi, n), 1)),
                  pl.BlockSpec((BLOCK, half_q), lambda bi, n: (row(bi, n), 2)),
                  pl.BlockSpec((BLOCK, LANES), lambda bi, n: (row(bi, n), k_blk + 2)),
                  pl.BlockSpec((s, LANES), lambda bi, n: (bi, k_blk)),
                  pl.BlockSpec((s, LANES), lambda bi, n: (bi, k_blk + 1)),
                  pl.BlockSpec((s, LANES), lambda bi, n: (bi, k_blk + 2))],
        out_specs=pl.BlockSpec((BLOCK, C_HEADS * C_HEAD_DIM), lambda bi, n: (row(bi, n), 0)),
        out_shape=jax.ShapeDtypeStruct((b * s, C_HEADS * C_HEAD_DIM), F32),
        scratch_shapes=[pltpu.VMEM((BLOCK, s), jnp.int32), pltpu.VMEM((BLOCK, s), jnp.int16)],
        compiler_params=_cparams("parallel", "arbitrary"), name="dsa",
    )(u, u, u, p1, p1, p1, p1)


ROUTER_TOKENS = 256
N_FULL_K0 = 8
SUB = 8


def _top_rows(work, k, exact):
    rows, cols = work.shape
    row = lax.broadcasted_iota(jnp.int32, (rows, cols), 0).astype(F32)
    vrow = lax.broadcasted_iota(jnp.int32, (k, cols), 0)
    rank = jnp.full((rows, cols), float(k), F32)
    vals = jnp.zeros((k, cols), F32)
    for r in range(k):
        m = jnp.max(work, axis=0, keepdims=True)
        hit = work == m
        if exact:
            first = jnp.min(jnp.where(hit, row, float(rows)), axis=0, keepdims=True)
            hit = row == first
        rank = jnp.where(hit, float(r), rank)
        work = jnp.where(hit, -jnp.inf, work)
        vals = jnp.where(vrow == r, m, vals)
    return rank, vals


def _count_ranked(rank, k):
    return jnp.sum(jnp.where(rank < float(k), 1.0, 0.0), axis=0, keepdims=True)


def _pair_counts_exact(v0, v1, k):
    cand = jnp.concatenate([v0[k0:k0 + 1] + v1 for k0 in range(N_FULL_K0)]
                           + [v0[N_FULL_K0:] + v1[0:1]], axis=0)
    rank_c, _ = _top_rows(cand, k, True)
    sel = rank_c < float(k)
    cnt = jnp.where(sel, 1.0, 0.0)
    n_rank = jnp.concatenate(
        [jnp.sum(cnt[k0 * k:(k0 + 1) * k], axis=0, keepdims=True) for k0 in range(N_FULL_K0)]
        + [cnt[N_FULL_K0 * k:]], axis=0)
    z = jnp.sum(jnp.where(sel, jnp.exp(cand - cand[0:1]), 0.0), axis=0, keepdims=True)
    return n_rank, z, jnp.sum(cnt, axis=0, keepdims=True)


def _pair_counts_fast(v0, v1, k):
    cols = v0.shape[1]
    row = lax.broadcasted_iota(jnp.int32, (SUB, cols), 0)
    tiles = []
    for k1 in range(SUB):
        n_valid = min(k // (k1 + 1), SUB)
        tiles.append(jnp.where(row < n_valid, v0[0:SUB] + v1[k1:k1 + 1], -jnp.inf))
    tiles.append(v0[0:1] + v1[SUB:])
    tiles.append(v0[SUB:] + v1[0:1])
    cand = jnp.concatenate(tiles, axis=0)
    rank_c, _ = _top_rows(cand, k, False)
    sel = rank_c < float(k)
    cnt = jnp.where(sel, 1.0, 0.0)
    low = cnt[0:SUB]
    for k1 in range(1, SUB):
        low = low + cnt[k1 * SUB:(k1 + 1) * SUB]
    tail = jnp.sum(cnt[SUB * SUB:SUB * SUB + SUB], axis=0, keepdims=True)
    low = low + jnp.where(row == 0, tail, 0.0)
    n_rank = jnp.concatenate([low, cnt[SUB * SUB + SUB:]], axis=0)
    z = jnp.sum(jnp.where(sel, jnp.exp(cand - (v0[0:1] + v1[0:1])), 0.0), axis=0, keepdims=True)
    return n_rank, z, jnp.sum(cnt, axis=0, keepdims=True)


def _router_kernel(q_ref, keys_ref, p0_ref, n0_ref, p1_ref, r1_ref):
    k = PEER_TOPK

    def head(h, carry):
        col0 = pl.multiple_of(h * (2 * PEER_HALF), LANES)
        col1 = pl.multiple_of(col0 + PEER_HALF, LANES)
        s0 = _dot_nt(keys_ref[2 * h], q_ref[:, pl.ds(col0, PEER_HALF)])
        s1 = _dot_nt(keys_ref[2 * h + 1], q_ref[:, pl.ds(col1, PEER_HALF)])

        def route(exact):
            rank0, v0 = _top_rows(s0, k, exact)
            rank1, v1 = _top_rows(s1, k, exact)
            n_rank, z, n_sel = (_pair_counts_exact if exact else _pair_counts_fast)(v0, v1, k)
            n0 = jnp.zeros_like(s0)
            for r in range(k):
                n0 = jnp.where(rank0 == float(r), n_rank[r:r + 1], n0)
            p0_ref[h] = jnp.where(rank0 < float(k), jnp.exp(s0 - v0[0:1]), 0.0) / z
            n0_ref[h] = n0
            p1_ref[h] = jnp.where(rank1 < float(k), jnp.exp(s1 - v1[0:1]), 0.0).astype(p1_ref.dtype)
            r1_ref[h] = rank1.astype(r1_ref.dtype)
            return _count_ranked(rank0, k) + _count_ranked(rank1, k) + n_sel

        n_marked = route(False)

        @pl.when(jnp.max(n_marked) > 3.0 * k)
        def _():
            route(True)

        return carry

    lax.fori_loop(0, PEER_HEADS, head, 0)


def _router(q, keys):
    t = q.shape[0]
    tb = ROUTER_TOKENS
    blk = pl.BlockSpec((PEER_HEADS, PEER_N_KEYS, tb), lambda i: (0, 0, i))
    f32s = jax.ShapeDtypeStruct((PEER_HEADS, PEER_N_KEYS, t), F32)
    bf16s = jax.ShapeDtypeStruct((PEER_HEADS, PEER_N_KEYS, t), BF16)
    return pl.pallas_call(
        _router_kernel, grid=(t // tb,),
        in_specs=[pl.BlockSpec((tb, q.shape[1]), lambda i: (i, 0)),
                  pl.BlockSpec(keys.shape, lambda i: (0, 0, 0))],
        out_specs=[blk, blk, blk, blk], out_shape=[f32s, f32s, bf16s, bf16s],
        compiler_params=_cparams("parallel"), name="peer_router",
    )(q, keys)


EXPERT_TOKENS = 512
EXPERT_CHUNK = 1024
ROWS_PER_CHUNK = EXPERT_CHUNK // PEER_N_KEYS


PACK = 16


def _gate_tile(pre_ref, p0_ref, n0_ref, p1_ref, r1_ref, h_ref, tile, cache):
    tb = pre_ref.shape[1]
    ii, rt = divmod(tile, PEER_N_KEYS // PACK)
    if ii not in cache:
        cache.clear()
        cache[ii] = (
            [jnp.broadcast_to(p0_ref[h, ii:ii + 1, :], (PACK, tb)).astype(BF16)
             for h in range(PEER_HEADS)],
            [jnp.broadcast_to(n0_ref[h, ii:ii + 1, :], (PACK, tb)).astype(BF16)
             for h in range(PEER_HEADS)])
    gates, cnts = cache[ii]
    js = slice(rt * PACK, (rt + 1) * PACK)
    w = None
    for h in range(PEER_HEADS):
        term = jnp.where(r1_ref[h, js, :] < cnts[h], p1_ref[h, js, :],
                         jnp.zeros((), BF16)) * gates[h]
        w = term if w is None else w + term
    rows = slice(tile * PACK, (tile + 1) * PACK)
    pre = pre_ref[rows, :]
    act = 0.5 * pre * (1.0 + lax.erf(pre * np.sqrt(0.5).astype(np.float32)))
    h_ref[rows, :] = act.astype(BF16) * w


def _expert_kernel(xn_ref, x_ref, u_ref, vt_ref, p0_ref, n0_ref, p1_ref, r1_ref, o_ref, acc_ref,
                   pre_ref, h_ref):
    c = pl.program_id(1)

    @pl.when(c == 0)
    def _():
        acc_ref[...] = jnp.zeros_like(acc_ref)

    pre_ref[...] = _dot_nt(u_ref[...], xn_ref[...])
    cache = {}
    for tile in range(EXPERT_CHUNK // PACK):
        _gate_tile(pre_ref, p0_ref, n0_ref, p1_ref, r1_ref, h_ref, tile, cache)
    acc_ref[...] += jnp.dot(vt_ref[...], h_ref[...], preferred_element_type=F32)

    @pl.when(c == pl.num_programs(1) - 1)
    def _():
        o_ref[...] = x_ref[...] + acc_ref[...].T


def _experts(xn, x, u, vt, p0, n0, p1, r1):
    t, d = x.shape
    tb, ec = EXPERT_TOKENS, EXPERT_CHUNK
    small = pl.BlockSpec((PEER_HEADS, ROWS_PER_CHUNK, tb), lambda i, c: (0, c, i))
    full = pl.BlockSpec((PEER_HEADS, PEER_N_KEYS, tb), lambda i, c: (0, 0, i))
    return pl.pallas_call(
        _expert_kernel, grid=(t // tb, PEER_EXPERTS // ec),
        in_specs=[pl.BlockSpec((tb, d), lambda i, c: (i, 0)),
                  pl.BlockSpec((tb, d), lambda i, c: (i, 0)),
                  pl.BlockSpec((ec, d), lambda i, c: (c, 0)),
                  pl.BlockSpec((d, ec), lambda i, c: (0, c)),
                  small, small, full, full],
        out_specs=pl.BlockSpec((tb, d), lambda i, c: (i, 0)),
        out_shape=jax.ShapeDtypeStruct((t, d), F32),
        scratch_shapes=[pltpu.VMEM((d, tb), F32), pltpu.VMEM((ec, tb), F32),
                        pltpu.VMEM((ec, tb), BF16)],
        compiler_params=_cparams("parallel", "arbitrary"), name="peer_experts",
    )(xn, x, u, vt, p0, n0, p1, r1)


def _peer(x, g, wq, keys, u, v):
    xq, xn = _proj(x, g, wq.astype(MXU_DTYPE), k_cols=x.shape[1], tm=512, tn=512, with_xn=True)
    keys2 = keys.reshape(2 * PEER_HEADS, PEER_N_KEYS, PEER_HALF).astype(MXU_DTYPE)
    p0, n0, p1, r1 = _router(xq, keys2)
    return _experts(xn, x, u.astype(MXU_DTYPE), v.astype(MXU_DTYPE).T, p0, n0, p1, r1)


def _norm_kernel(x_ref, g_ref, o_ref):
    x = x_ref[...]
    ms = jnp.mean(x * x, axis=-1, keepdims=True)
    o_ref[...] = x * lax.rsqrt(ms + EPS) * g_ref[...]


def _norm(x, g, tm=512):
    t, d = x.shape
    return pl.pallas_call(
        _norm_kernel, grid=(t // tm,),
        in_specs=[pl.BlockSpec((tm, d), lambda i: (i, 0)), pl.BlockSpec((1, d), lambda i: (0, 0))],
        out_specs=pl.BlockSpec((tm, d), lambda i: (i, 0)),
        out_shape=jax.ShapeDtypeStruct((t, d), F32),
        compiler_params=_cparams("parallel"), name="final_norm",
    )(x, g.reshape(1, d))


def kernel(x, positions, norm_mix0, w_in0, sinks0, conv_w0, w_out0, norm_ffn0, peer_wq0, peer_keys0,
           peer_u0, peer_v0, norm_mix1, w_in1, g_qa1, w_uq1, w_out1, norm_ffn1, peer_wq1,
           peer_keys1, peer_u1, peer_v1, norm_final):
    b, s, d = x.shape
    t = b * s
    xf = x.reshape(t, d)
    pos = positions.reshape(t)
    tab_a = _rope_table(pos, ((0, 64), (64, 64)), 8)
    tab_b = _rope_table(pos, ((0, 128),), 16)
    tab_c = _rope_table(pos, ((0, 64),), 8)

    tn0 = 256
    n_rope = (A_Q_COLS + A_KV_COLS) // tn0
    groups0 = (((0, n_rope), (0, 0)), ((n_rope, w_in0.shape[1] // tn0), (-1, -1)))
    p0 = _proj(xf, norm_mix0, w_in0.astype(MXU_DTYPE), k_cols=d, tm=512, tn=tn0,
               tabs=(tab_a,), halves=(8,), groups=groups0)
    a_out = _swa(p0, sinks0, b, s)
    b_out = _conv(p0, conv_w0, s)
    w_o = w_out0.astype(MXU_DTYPE)
    x1 = _mm_res([a_out, b_out], [w_o[:A_Q_COLS], w_o[A_Q_COLS:]], xf, tm=512, tn=512)
    x2 = _peer(x1, norm_ffn0, peer_wq0, peer_keys0, peer_u0, peer_v0)

    w1 = jnp.pad(w_in1, ((0, 0), (0, IN1_COLS_PADDED - w_in1.shape[1]))).astype(MXU_DTYPE)
    k_chunk = C_Q_RANK // LANES
    cfg1 = tuple({k_chunk: 0, k_chunk + 2: 1}.get(c, -1) for c in range(IN1_COLS_PADDED // LANES))
    p1 = _proj(x2, norm_mix1, w1, k_cols=d, tm=512, tn=IN1_COLS_PADDED,
               tabs=(tab_b, tab_c), halves=(16, 8), groups=(((0, 1), cfg1),))
    n_q = C_HEADS * C_HEAD_DIM // LANES
    n_iq = IDX_HEADS * IDX_HEAD_DIM // LANES
    cfg_u = (0,) * n_q + (1,) * n_iq
    u = _proj(p1, g_qa1, w_uq1.astype(MXU_DTYPE), k_cols=C_Q_RANK, tm=256,
              tn=(n_q + n_iq) * LANES, tabs=(tab_b, tab_a), halves=(16, 8),
              groups=(((0, 1), cfg_u),))
    o = _dsa(u, p1, b, s)
    x3 = _mm_res([o], [w_out1.astype(MXU_DTYPE)], x2, tm=512, tn=512)
    x4 = _peer(x3, norm_ffn1, peer_wq1, peer_keys1, peer_u1, peer_v1)
    return _norm(x4, norm_final).reshape(b, s, d)
```

```python
import functools

import numpy as np
import jax
import jax.numpy as jnp
from jax import lax
from jax.experimental import pallas as pl
from jax.experimental.pallas import tpu as pltpu

F32 = jnp.float32
BF16 = jnp.bfloat16
MXU_DTYPE = jnp.bfloat16

LANES = 128
EPS = 1e-5
ROPE_THETA = 500000.0
BLOCK = 128
NEG_INF = -1e30
INT_MIN = -(2 ** 31)

SWA_WINDOW = 128
A_Q_HEADS, A_KV_HEADS, A_HEAD_DIM = 16, 4, 64
A_Q_COLS = A_Q_HEADS * A_HEAD_DIM
A_KV_COLS = A_KV_HEADS * A_HEAD_DIM
B_WIDTH = 1024
C_Q_RANK, C_HEADS, C_HEAD_DIM = 512, 16, 128
IDX_HEADS, IDX_HEAD_DIM, INDEX_TOPK = 16, 64, 256
IN1_COLS_PADDED = 896
PEER_HEADS, PEER_N_KEYS, PEER_TOPK, PEER_HALF = 8, 128, 16, 128
PEER_EXPERTS = PEER_N_KEYS * PEER_N_KEYS

VMEM_LIMIT = 56 * 1024 * 1024


def _cparams(*sem):
    return pltpu.CompilerParams(dimension_semantics=sem, vmem_limit_bytes=VMEM_LIMIT)


def _col_tiles(w, tn):
    k, n = w.shape
    return w.reshape(k, n // tn, tn).transpose(1, 0, 2)


def _dot(a, b):
    return jnp.dot(a.astype(MXU_DTYPE), b.astype(MXU_DTYPE), preferred_element_type=F32)


def _dot_nt(a, b):
    return lax.dot_general(a.astype(MXU_DTYPE), b.astype(MXU_DTYPE),
                           (((1,), (1,)), ((), ())), preferred_element_type=F32)


def _rope_table(pos, heads, half):
    rot = 2 * half
    inv = ROPE_THETA ** (-jnp.arange(half, dtype=F32) * 2.0 / rot)
    freq = np.zeros(LANES, np.int32)
    rotated = np.zeros(LANES, bool)
    upper = np.zeros(LANES, bool)
    for lane0, dh in heads:
        assert dh // 8 == half
        freq[lane0:lane0 + rot] = np.arange(rot) % half
        rotated[lane0:lane0 + rot] = True
        upper[lane0 + half:lane0 + rot] = True
    inv_lane = jnp.where(rotated, inv[freq], 0.0)
    ang = pos.astype(F32)[:, None] * inv_lane[None, :]
    cos, sin = jnp.cos(ang), jnp.sin(ang)
    s1 = jnp.where(upper, sin, 0.0)
    s2 = jnp.where(rotated & ~upper, -sin, 0.0)
    return jnp.stack([cos, s1, s2])


def _rope_chunk(xc, tab_ref, half):
    return (xc * tab_ref[0] + pltpu.roll(xc, half, 1) * tab_ref[1]
            + pltpu.roll(xc, LANES - half, 1) * tab_ref[2])


def _proj_kernel(*refs, n_tabs, halves, groups, tn, with_xn):
    x_ref, g_ref, w_ref = refs[:3]
    tab_refs = refs[3:3 + n_tabs]
    o_ref = refs[3 + n_tabs]
    xn_out = refs[4 + n_tabs] if with_xn else None
    xn_ref = refs[-1]
    j = pl.program_id(1)

    @pl.when(j == 0)
    def _():
        x = x_ref[...]
        ms = jnp.mean(x * x, axis=-1, keepdims=True)
        xn = (x * lax.rsqrt(ms + EPS) * g_ref[...]).astype(MXU_DTYPE)
        xn_ref[...] = xn
        if with_xn:
            xn_out[...] = xn

    acc = jnp.dot(xn_ref[...], w_ref[...], preferred_element_type=F32)

    for (j_lo, j_hi), cfg in groups:
        @pl.when((j >= j_lo) & (j < j_hi))
        def _(cfg=cfg):
            for c in range(tn // LANES):
                xc = acc[:, c * LANES:(c + 1) * LANES]
                if cfg[c] >= 0:
                    xc = _rope_chunk(xc, tab_refs[cfg[c]], halves[cfg[c]])
                o_ref[:, c * LANES:(c + 1) * LANES] = xc


def _proj(x, g, w, *, k_cols, tm, tn, tabs=(), halves=(), groups=None, with_xn=False):
    t = x.shape[0]
    n = w.shape[1]
    nj = n // tn
    if groups is None:
        groups = (((0, nj), (-1,) * (tn // LANES)),)
    kern = functools.partial(_proj_kernel, n_tabs=len(tabs), halves=tuple(halves),
                             groups=tuple(groups), tn=tn, with_xn=with_xn)
    in_specs = [pl.BlockSpec((tm, k_cols), lambda i, j: (i, 0)),
                pl.BlockSpec((1, k_cols), lambda i, j: (0, 0)),
                pl.BlockSpec((None, k_cols, tn), lambda i, j: (j, 0, 0))]
    in_specs += [pl.BlockSpec((3, tm, LANES), lambda i, j: (0, i, 0)) for _ in tabs]
    out_shape = [jax.ShapeDtypeStruct((t, n), F32)]
    out_specs = [pl.BlockSpec((tm, tn), lambda i, j: (i, j))]
    if with_xn:
        out_shape.append(jax.ShapeDtypeStruct((t, k_cols), MXU_DTYPE))
        out_specs.append(pl.BlockSpec((tm, k_cols), lambda i, j: (i, 0)))
    res = pl.pallas_call(
        kern, grid=(t // tm, nj), in_specs=in_specs, out_specs=out_specs, out_shape=out_shape,
        scratch_shapes=[pltpu.VMEM((tm, k_cols), MXU_DTYPE)],
        compiler_params=_cparams("parallel", "arbitrary"), name="norm_proj",
    )(x, g.reshape(1, k_cols), _col_tiles(w, tn), *tabs)
    return res if with_xn else res[0]


def _mm_res_kernel(*refs, n_lhs):
    a_refs = refs[:n_lhs]
    w_refs = refs[n_lhs:2 * n_lhs]
    x_ref = refs[2 * n_lhs]
    o_ref = refs[2 * n_lhs + 1]
    ab_refs = refs[2 * n_lhs + 2:]

    @pl.when(pl.program_id(1) == 0)
    def _():
        for a_ref, ab_ref in zip(a_refs, ab_refs):
            ab_ref[...] = a_ref[...].astype(MXU_DTYPE)

    acc = x_ref[...]
    for ab_ref, w_ref in zip(ab_refs, w_refs):
        acc = acc + jnp.dot(ab_ref[...], w_ref[...], preferred_element_type=F32)
    o_ref[...] = acc


def _mm_res(lhs, ws, x, *, tm, tn):
    t, d = x.shape
    n_lhs = len(lhs)
    in_specs = [pl.BlockSpec((tm, a.shape[1]), lambda i, j: (i, 0)) for a in lhs]
    in_specs += [pl.BlockSpec((None, w.shape[0], tn), lambda i, j: (j, 0, 0)) for w in ws]
    in_specs += [pl.BlockSpec((tm, tn), lambda i, j: (i, j))]
    return pl.pallas_call(
        functools.partial(_mm_res_kernel, n_lhs=n_lhs), grid=(t // tm, d // tn),
        in_specs=in_specs, out_specs=pl.BlockSpec((tm, tn), lambda i, j: (i, j)),
        out_shape=jax.ShapeDtypeStruct((t, d), F32),
        scratch_shapes=[pltpu.VMEM((tm, a.shape[1]), MXU_DTYPE) for a in lhs],
        compiler_params=_cparams("parallel", "arbitrary"), name="out_proj",
    )(*lhs, *[_col_tiles(w, tn) for w in ws], x)


def _swa_kernel(sink_ref, q_ref, kp_ref, kc_ref, vp_ref, vc_ref, o_ref):
    n = pl.program_id(1)
    g_sz = A_Q_HEADS // A_KV_HEADS
    dh = A_HEAD_DIM
    q = q_ref[...]
    k2 = jnp.concatenate([kp_ref[...], kc_ref[...]], axis=0).astype(MXU_DTYPE)
    v2 = jnp.concatenate([vp_ref[...], vc_ref[...]], axis=0).astype(MXU_DTYPE)
    qi = lax.broadcasted_iota(jnp.int32, (BLOCK, 2 * BLOCK), 0)
    kj = lax.broadcasted_iota(jnp.int32, (BLOCK, 2 * BLOCK), 1)
    diff = qi + BLOCK - kj
    valid = (diff >= 0) & (diff < SWA_WINDOW) & ((kj >= BLOCK) | (n > 0))
    scale = dh ** -0.5
    for h in range(A_KV_HEADS):
        kh = k2[:, h * dh:(h + 1) * dh]
        vh = v2[:, h * dh:(h + 1) * dh]
        for g in range(g_sz):
            hq = h * g_sz + g
            s = _dot_nt(q[:, hq * dh:(hq + 1) * dh], kh) * scale
            s = jnp.where(valid, s, NEG_INF)
            sink = sink_ref[hq]
            m = jnp.maximum(jnp.max(s, axis=-1, keepdims=True), sink)
            p = jnp.exp(s - m)
            denom = jnp.sum(p, axis=-1, keepdims=True) + jnp.exp(sink - m)
            o_ref[:, hq * dh:(hq + 1) * dh] = _dot(p, vh) / denom


def _swa(p0, sinks, b, s):
    nb = s // BLOCK
    kv_w = A_KV_COLS
    k_blk = A_Q_COLS // kv_w
    v_blk = k_blk + 1
    cur = lambda bi, n: bi * nb + n
    prev = lambda bi, n: bi * nb + jnp.maximum(n - 1, 0)
    return pl.pallas_call(
        _swa_kernel, grid=(b, nb),
        in_specs=[pl.BlockSpec(memory_space=pltpu.SMEM),
                  pl.BlockSpec((BLOCK, A_Q_COLS), lambda bi, n: (cur(bi, n), 0)),
                  pl.BlockSpec((BLOCK, kv_w), lambda bi, n: (prev(bi, n), k_blk)),
                  pl.BlockSpec((BLOCK, kv_w), lambda bi, n: (cur(bi, n), k_blk)),
                  pl.BlockSpec((BLOCK, kv_w), lambda bi, n: (prev(bi, n), v_blk)),
                  pl.BlockSpec((BLOCK, kv_w), lambda bi, n: (cur(bi, n), v_blk))],
        out_specs=pl.BlockSpec((BLOCK, A_Q_COLS), lambda bi, n: (cur(bi, n), 0)),
        out_shape=jax.ShapeDtypeStruct((b * s, A_Q_COLS), F32),
        compiler_params=_cparams("parallel", "arbitrary"), name="swa",
    )(sinks, p0, p0, p0, p0, p0)


CONV_ROWS = 256
CONV_COLS = 512
HALO = 8


def _conv_kernel(bg_ref, cg_ref, hx_ref, cgh_ref, hxh_ref, w_ref, o_ref, *, tiles_per_seq):
    i = pl.program_id(0)
    z = cg_ref[...] * hx_ref[...]
    first = (i % tiles_per_seq) == 0
    zh = jnp.where(first, 0.0, cgh_ref[...] * hxh_ref[...])
    row = lax.broadcasted_iota(jnp.int32, z.shape, 0)
    z1 = pltpu.roll(z, 1, 0)
    z1 = jnp.where(row == 0, zh[HALO - 1:HALO], z1)
    z2 = pltpu.roll(z, 2, 0)
    z2 = jnp.where(row == 0, zh[HALO - 2:HALO - 1], jnp.where(row == 1, zh[HALO - 1:HALO], z2))
    w = w_ref[...]
    o_ref[...] = bg_ref[...] * (w[0:1] * z2 + w[1:2] * z1 + w[2:3] * z)


def _conv(p0, conv_w, s):
    t = p0.shape[0]
    col0 = A_Q_COLS + 2 * A_KV_COLS
    bg_blk, cg_blk, hx_blk = (col0 // CONV_COLS, (col0 + B_WIDTH) // CONV_COLS,
                              (col0 + 2 * B_WIDTH) // CONV_COLS)
    rh = CONV_ROWS // HALO
    halo_row = lambda i: jnp.maximum(i * rh - 1, 0)
    return pl.pallas_call(
        functools.partial(_conv_kernel, tiles_per_seq=s // CONV_ROWS),
        grid=(t // CONV_ROWS, B_WIDTH // CONV_COLS),
        in_specs=[pl.BlockSpec((CONV_ROWS, CONV_COLS), lambda i, j: (i, bg_blk + j)),
                  pl.BlockSpec((CONV_ROWS, CONV_COLS), lambda i, j: (i, cg_blk + j)),
                  pl.BlockSpec((CONV_ROWS, CONV_COLS), lambda i, j: (i, hx_blk + j)),
                  pl.BlockSpec((HALO, CONV_COLS), lambda i, j: (halo_row(i), cg_blk + j)),
                  pl.BlockSpec((HALO, CONV_COLS), lambda i, j: (halo_row(i), hx_blk + j)),
                  pl.BlockSpec((3, CONV_COLS), lambda i, j: (0, j))],
        out_specs=pl.BlockSpec((CONV_ROWS, CONV_COLS), lambda i, j: (i, j)),
        out_shape=jax.ShapeDtypeStruct((t, B_WIDTH), F32),
        compiler_params=_cparams("parallel", "parallel"), name="gated_conv",
    )(p0, p0, p0, p0, p0, conv_w)


DSA_WIDTHS = 4


def _dsa_block(qa_ref, qb_ref, iq_ref, iwq_ref, k_ref, v_ref, ikw_ref, o_ref, key_ref, half_ref,
               n, w, k_sel):
    ik = ikw_ref[0:w, 0:IDX_HEAD_DIM].astype(MXU_DTYPE)
    iw = iwq_ref[:, IDX_HEAD_DIM:IDX_HEAD_DIM + IDX_HEADS] * (
        (IDX_HEADS ** -0.5) * (IDX_HEAD_DIM ** -0.5))
    iq = iq_ref[...]
    score = jnp.zeros((BLOCK, w), F32)
    for h in range(IDX_HEADS):
        lg = _dot_nt(iq[:, h * IDX_HEAD_DIM:(h + 1) * IDX_HEAD_DIM], ik)
        score = score + iw[:, h:h + 1] * jnp.maximum(lg, 0.0)
    score = score + 0.0
    qpos = n * BLOCK + lax.broadcasted_iota(jnp.int32, (BLOCK, w), 0)
    kpos = lax.broadcasted_iota(jnp.int32, (BLOCK, w), 1)
    bits = pltpu.bitcast(score, jnp.int32)
    key = jnp.where(bits < 0, bits ^ jnp.int32(0x7FFFFFFF), bits)
    key = jnp.where(kpos <= qpos, key, INT_MIN)
    key_ref[:, 0:w] = key
    half_ref[:, 0:w] = jnp.right_shift(key, 16).astype(jnp.int16)

    def search16():
        def count_ge(t):
            t16 = t.astype(jnp.int16)
            acc = None
            for c in range(w // LANES):
                one = jnp.where(half_ref[:, c * LANES:(c + 1) * LANES] >= t16,
                                jnp.int16(1), jnp.int16(0))
                acc = one if acc is None else acc + one
            return jnp.sum(acc.astype(F32), axis=-1, keepdims=True)

        kf = float(k_sel)
        t0 = jnp.where(count_ge(jnp.zeros((BLOCK, 1), jnp.int32)) >= kf,
                       jnp.int32(0), jnp.int32(-2 ** 15))

        def bit_step(it, t):
            cand = t + jnp.left_shift(jnp.int32(1), jnp.int32(14) - it)
            return jnp.where(count_ge(cand) >= kf, cand, t)

        return lax.fori_loop(0, 15, bit_step, t0)

    t_hi = search16()
    hi = half_ref[:, 0:w].astype(jnp.int32)
    lo = jnp.bitwise_and(key_ref[:, 0:w], 0xFFFF) - 2 ** 15
    half_ref[:, 0:w] = jnp.where(hi > t_hi, 2 ** 15 - 1,
                                 jnp.where(hi < t_hi, -2 ** 15, lo)).astype(jnp.int16)
    t_lo = search16()
    thr = jnp.maximum(t_hi * 2 ** 16 + (t_lo + 2 ** 15), INT_MIN + 1)
    sel = key_ref[:, 0:w] >= thr

    kk = k_ref[0:w, :].astype(MXU_DTYPE)
    lane = lax.broadcasted_iota(jnp.int32, (w, LANES), 1)
    v_aug = jnp.concatenate([v_ref[0:w, :].astype(MXU_DTYPE),
                             jnp.where(lane == 0, 1.0, 0.0).astype(MXU_DTYPE)], axis=1)
    scale = C_HEAD_DIM ** -0.5
    for h in range(C_HEADS):
        q_r = qa_ref if h < C_HEADS // 2 else qb_ref
        hh = h % (C_HEADS // 2)
        s = _dot_nt(q_r[:, hh * C_HEAD_DIM:(hh + 1) * C_HEAD_DIM] * scale, kk)
        s = jnp.where(sel, s, NEG_INF)
        p = jnp.exp(s - jnp.max(s, axis=-1, keepdims=True))
        pv = _dot(p, v_aug)
        o_ref[:, h * C_HEAD_DIM:(h + 1) * C_HEAD_DIM] = (
            pv[:, :C_HEAD_DIM] / pv[:, C_HEAD_DIM:C_HEAD_DIM + 1])


def _dsa_kernel(qa_ref, qb_ref, iq_ref, iwq_ref, k_ref, v_ref, ikw_ref, o_ref, key_ref, half_ref,
                *, k_sel):
    n = pl.program_id(1)
    chunk = k_ref.shape[0] // DSA_WIDTHS
    n_chunks = (n * BLOCK + BLOCK - 1) // chunk + 1
    for wi in range(1, DSA_WIDTHS + 1):
        @pl.when(n_chunks == wi)
        def _(wi=wi):
            _dsa_block(qa_ref, qb_ref, iq_ref, iwq_ref, k_ref, v_ref, ikw_ref, o_ref, key_ref,
                       half_ref, n, wi * chunk, k_sel)


def _dsa(u, p1, b, s):
    nb = s // BLOCK
    k_sel = min(INDEX_TOPK, s // 4)
    half_q = C_HEADS * C_HEAD_DIM // 2
    k_blk = C_Q_RANK // LANES
    row = lambda bi, n: bi * nb + n
    return pl.pallas_call(
        functools.partial(_dsa_kernel, k_sel=k_sel), grid=(b, nb),
        in_specs=[pl.BlockSpec((BLOCK, half_q), lambda bi, n: (row(bi, n), 0)),
                  pl.BlockSpec((BLOCK, half_q), lambda bi, n: (row(bi, n), 1)),
                  pl.BlockSpec((BLOCK, half_q), lambda bi, n: (row(bi, n), 2)),
                  pl.BlockSpec((BLOCK, LANES), lambda bi, n: (row(bi, n), k_blk + 2)),
                  pl.BlockSpec((s, LANES), lambda bi, n: (bi, k_blk)),
                  pl.BlockSpec((s, LANES), lambda bi, n: (bi, k_blk + 1)),
                  pl.BlockSpec((s, LANES), lambda bi, n: (bi, k_blk + 2))],
        out_specs=pl.BlockSpec((BLOCK, C_HEADS * C_HEAD_DIM), lambda bi, n: (row(bi, n), 0)),
        out_shape=jax.ShapeDtypeStruct((b * s, C_HEADS * C_HEAD_DIM), F32),
        scratch_shapes=[pltpu.VMEM((BLOCK, s), jnp.int32), pltpu.VMEM((BLOCK, s), jnp.int16)],
        compiler_params=_cparams("parallel", "arbitrary"), name="dsa",
    )(u, u, u, p1, p1, p1, p1)


ROUTER_TOKENS = 256
N_FULL_K0 = 8
SUB = 8


def _top_rows(work, k, exact):
    rows, cols = work.shape
    row = lax.broadcasted_iota(jnp.int32, (rows, cols), 0).astype(F32)
    vrow = lax.broadcasted_iota(jnp.int32, (k, cols), 0)
    rank = jnp.full((rows, cols), float(k), F32)
    vals = jnp.zeros((k, cols), F32)
    for r in range(k):
        m = jnp.max(work, axis=0, keepdims=True)
        hit = work == m
        if exact:
            first = jnp.min(jnp.where(hit, row, float(rows)), axis=0, keepdims=True)
            hit = row == first
        rank = jnp.where(hit, float(r), rank)
        work = jnp.where(hit, -jnp.inf, work)
        vals = jnp.where(vrow == r, m, vals)
    return rank, vals


def _count_ranked(rank, k):
    return jnp.sum(jnp.where(rank < float(k), 1.0, 0.0), axis=0, keepdims=True)


def _pair_counts_exact(v0, v1, k):
    cand = jnp.concatenate([v0[k0:k0 + 1] + v1 for k0 in range(N_FULL_K0)]
                           + [v0[N_FULL_K0:] + v1[0:1]], axis=0)
    rank_c, _ = _top_rows(cand, k, True)
    sel = rank_c < float(k)
    cnt = jnp.where(sel, 1.0, 0.0)
    n_rank = jnp.concatenate(
        [jnp.sum(cnt[k0 * k:(k0 + 1) * k], axis=0, keepdims=True) for k0 in range(N_FULL_K0)]
        + [cnt[N_FULL_K0 * k:]], axis=0)
    z = jnp.sum(jnp.where(sel, jnp.exp(cand - cand[0:1]), 0.0), axis=0, keepdims=True)
    return n_rank, z, jnp.sum(cnt, axis=0, keepdims=True)


def _pair_counts_fast(v0, v1, k):
    cols = v0.shape[1]
    row = lax.broadcasted_iota(jnp.int32, (SUB, cols), 0)
    tiles = []
    for k1 in range(SUB):
        n_valid = min(k // (k1 + 1), SUB)
        tiles.append(jnp.where(row < n_valid, v0[0:SUB] + v1[k1:k1 + 1], -jnp.inf))
    tiles.append(v0[0:1] + v1[SUB:])
    tiles.append(v0[SUB:] + v1[0:1])
    cand = jnp.concatenate(tiles, axis=0)
    rank_c, _ = _top_rows(cand, k, False)
    sel = rank_c < float(k)
    cnt = jnp.where(sel, 1.0, 0.0)
    low = cnt[0:SUB]
    for k1 in range(1, SUB):
        low = low + cnt[k1 * SUB:(k1 + 1) * SUB]
    tail = jnp.sum(cnt[SUB * SUB:SUB * SUB + SUB], axis=0, keepdims=True)
    low = low + jnp.where(row == 0, tail, 0.0)
    n_rank = jnp.concatenate([low, cnt[SUB * SUB + SUB:]], axis=0)
    z = jnp.sum(jnp.where(sel, jnp.exp(cand - (v0[0:1] + v1[0:1])), 0.0), axis=0, keepdims=True)
    return n_rank, z, jnp.sum(cnt, axis=0, keepdims=True)


def _router_kernel(q_ref, keys_ref, p0_ref, n0_ref, p1_ref, r1_ref):
    k = PEER_TOPK

    def head(h, carry):
        col0 = pl.multiple_of(h * (2 * PEER_HALF), LANES)
        col1 = pl.multiple_of(col0 + PEER_HALF, LANES)
        s0 = _dot_nt(keys_ref[2 * h], q_ref[:, pl.ds(col0, PEER_HALF)])
        s1 = _dot_nt(keys_ref[2 * h + 1], q_ref[:, pl.ds(col1, PEER_HALF)])

        def route(exact):
            rank0, v0 = _top_rows(s0, k, exact)
            rank1, v1 = _top_rows(s1, k, exact)
            n_rank, z, n_sel = (_pair_counts_exact if exact else _pair_counts_fast)(v0, v1, k)
            n0 = jnp.zeros_like(s0)
            for r in range(k):
                n0 = jnp.where(rank0 == float(r), n_rank[r:r + 1], n0)
            p0_ref[h] = jnp.where(rank0 < float(k), jnp.exp(s0 - v0[0:1]), 0.0) / z
            n0_ref[h] = n0
            p1_ref[h] = jnp.where(rank1 < float(k), jnp.exp(s1 - v1[0:1]), 0.0).astype(p1_ref.dtype)
            r1_ref[h] = rank1.astype(r1_ref.dtype)
            return _count_ranked(rank0, k) + _count_ranked(rank1, k) + n_sel

        n_marked = route(False)

        @pl.when(jnp.max(n_marked) > 3.0 * k)
        def _():
            route(True)

        return carry

    lax.fori_loop(0, PEER_HEADS, head, 0)


def _router(q, keys):
    t = q.shape[0]
    tb = ROUTER_TOKENS
    blk = pl.BlockSpec((PEER_HEADS, PEER_N_KEYS, tb), lambda i: (0, 0, i))
    f32s = jax.ShapeDtypeStruct((PEER_HEADS, PEER_N_KEYS, t), F32)
    bf16s = jax.ShapeDtypeStruct((PEER_HEADS, PEER_N_KEYS, t), BF16)
    return pl.pallas_call(
        _router_kernel, grid=(t // tb,),
        in_specs=[pl.BlockSpec((tb, q.shape[1]), lambda i: (i, 0)),
                  pl.BlockSpec(keys.shape, lambda i: (0, 0, 0))],
        out_specs=[blk, blk, blk, blk], out_shape=[f32s, f32s, bf16s, bf16s],
        compiler_params=_cparams("parallel"), name="peer_router",
    )(q, keys)


EXPERT_TOKENS = 512
EXPERT_CHUNK = 1024
ROWS_PER_CHUNK = EXPERT_CHUNK // PEER_N_KEYS


PACK = 16


def _gate_tile(pre_ref, p0_ref, n0_ref, p1_ref, r1_ref, h_ref, tile, cache):
    tb = pre_ref.shape[1]
    ii, rt = divmod(tile, PEER_N_KEYS // PACK)
    if ii not in cache:
        cache.clear()
        cache[ii] = (
            [jnp.broadcast_to(p0_ref[h, ii:ii + 1, :], (PACK, tb)).astype(BF16)
             for h in range(PEER_HEADS)],
            [jnp.broadcast_to(n0_ref[h, ii:ii + 1, :], (PACK, tb)).astype(BF16)
             for h in range(PEER_HEADS)])
    gates, cnts = cache[ii]
    js = slice(rt * PACK, (rt + 1) * PACK)
    w = None
    for h in range(PEER_HEADS):
        term = jnp.where(r1_ref[h, js, :] < cnts[h], p1_ref[h, js, :],
                         jnp.zeros((), BF16)) * gates[h]
        w = term if w is None else w + term
    rows = slice(tile * PACK, (tile + 1) * PACK)
    pre = pre_ref[rows, :]
    act = 0.5 * pre * (1.0 + lax.erf(pre * np.sqrt(0.5).astype(np.float32)))
    h_ref[rows, :] = act.astype(BF16) * w


def _expert_kernel(xn_ref, x_ref, u_ref, vt_ref, p0_ref, n0_ref, p1_ref, r1_ref, o_ref, acc_ref,
                   pre_ref, h_ref):
    c = pl.program_id(1)

    @pl.when(c == 0)
    def _():
        acc_ref[...] = jnp.zeros_like(acc_ref)

    pre_ref[...] = _dot_nt(u_ref[...], xn_ref[...])
    cache = {}
    for tile in range(EXPERT_CHUNK // PACK):
        _gate_tile(pre_ref, p0_ref, n0_ref, p1_ref, r1_ref, h_ref, tile, cache)
    acc_ref[...] += jnp.dot(vt_ref[...], h_ref[...], preferred_element_type=F32)

    @pl.when(c == pl.num_programs(1) - 1)
    def _():
        o_ref[...] = x_ref[...] + acc_ref[...].T


def _experts(xn, x, u, vt, p0, n0, p1, r1):
    t, d = x.shape
    tb, ec = EXPERT_TOKENS, EXPERT_CHUNK
    small = pl.BlockSpec((PEER_HEADS, ROWS_PER_CHUNK, tb), lambda i, c: (0, c, i))
    full = pl.BlockSpec((PEER_HEADS, PEER_N_KEYS, tb), lambda i, c: (0, 0, i))
    return pl.pallas_call(
        _expert_kernel, grid=(t // tb, PEER_EXPERTS // ec),
        in_specs=[pl.BlockSpec((tb, d), lambda i, c: (i, 0)),
                  pl.BlockSpec((tb, d), lambda i, c: (i, 0)),
                  pl.BlockSpec((ec, d), lambda i, c: (c, 0)),
                  pl.BlockSpec((None, d, ec), lambda i, c: (c, 0, 0)),
                  small, small, full, full],
        out_specs=pl.BlockSpec((tb, d), lambda i, c: (i, 0)),
        out_shape=jax.ShapeDtypeStruct((t, d), F32),
        scratch_shapes=[pltpu.VMEM((d, tb), F32), pltpu.VMEM((ec, tb), F32),
                        pltpu.VMEM((ec, tb), BF16)],
        compiler_params=_cparams("parallel", "arbitrary"), name="peer_experts",
    )(xn, x, u, vt, p0, n0, p1, r1)


def _peer(x, g, wq, keys, u, v):
    xq, xn = _proj(x, g, wq.astype(MXU_DTYPE), k_cols=x.shape[1], tm=512, tn=512, with_xn=True)
    keys2 = keys.reshape(2 * PEER_HEADS, PEER_N_KEYS, PEER_HALF).astype(MXU_DTYPE)
    p0, n0, p1, r1 = _router(xq, keys2)
    nc = PEER_EXPERTS // EXPERT_CHUNK
    vt = v.astype(MXU_DTYPE).reshape(nc, EXPERT_CHUNK, -1).transpose(0, 2, 1)
    return _experts(xn, x, u.astype(MXU_DTYPE), vt, p0, n0, p1, r1)


def _norm_kernel(x_ref, g_ref, o_ref):
    x = x_ref[...]
    ms = jnp.mean(x * x, axis=-1, keepdims=True)
    o_ref[...] = x * lax.rsqrt(ms + EPS) * g_ref[...]


def _norm(x, g, tm=512):
    t, d = x.shape
    return pl.pallas_call(
        _norm_kernel, grid=(t // tm,),
        in_specs=[pl.BlockSpec((tm, d), lambda i: (i, 0)), pl.BlockSpec((1, d), lambda i: (0, 0))],
        out_specs=pl.BlockSpec((tm, d), lambda i: (i, 0)),
        out_shape=jax.ShapeDtypeStruct((t, d), F32),
        compiler_params=_cparams("parallel"), name="final_norm",
    )(x, g.reshape(1, d))


def kernel(x, positions, norm_mix0, w_in0, sinks0, conv_w0, w_out0, norm_ffn0, peer_wq0, peer_keys0,
           peer_u0, peer_v0, norm_mix1, w_in1, g_qa1, w_uq1, w_out1, norm_ffn1, peer_wq1,
           peer_keys1, peer_u1, peer_v1, norm_final):
    b, s, d = x.shape
    t = b * s
    xf = x.reshape(t, d)
    pos = positions.reshape(t)
    tab_a = _rope_table(pos, ((0, 64), (64, 64)), 8)
    tab_b = _rope_table(pos, ((0, 128),), 16)
    tab_c = _rope_table(pos, ((0, 64),), 8)

    tn0 = 512
    groups0 = (((0, 2), (0, 0, 0, 0)), ((2, 3), (0, 0, -1, -1)),
               ((3, w_in0.shape[1] // tn0), (-1, -1, -1, -1)))
    p0 = _proj(xf, norm_mix0, w_in0.astype(MXU_DTYPE), k_cols=d, tm=512, tn=tn0,
               tabs=(tab_a,), halves=(8,), groups=groups0)
    a_out = _swa(p0, sinks0, b, s)
    b_out = _conv(p0, conv_w0, s)
    w_o = w_out0.astype(MXU_DTYPE)
    x1 = _mm_res([a_out, b_out], [w_o[:A_Q_COLS], w_o[A_Q_COLS:]], xf, tm=512, tn=512)
    x2 = _peer(x1, norm_ffn0, peer_wq0, peer_keys0, peer_u0, peer_v0)

    w1 = jnp.pad(w_in1, ((0, 0), (0, IN1_COLS_PADDED - w_in1.shape[1]))).astype(MXU_DTYPE)
    k_chunk = C_Q_RANK // LANES
    cfg1 = tuple({k_chunk: 0, k_chunk + 2: 1}.get(c, -1) for c in range(IN1_COLS_PADDED // LANES))
    p1 = _proj(x2, norm_mix1, w1, k_cols=d, tm=512, tn=IN1_COLS_PADDED,
               tabs=(tab_b, tab_c), halves=(16, 8), groups=(((0, 1), cfg1),))
    n_q = C_HEADS * C_HEAD_DIM // LANES
    n_iq = IDX_HEADS * IDX_HEAD_DIM // LANES
    cfg_u = (0,) * n_q + (1,) * n_iq
    u = _proj(p1, g_qa1, w_uq1.astype(MXU_DTYPE), k_cols=C_Q_RANK, tm=256,
              tn=(n_q + n_iq) * LANES, tabs=(tab_b, tab_a), halves=(16, 8),
              groups=(((0, 1), cfg_u),))
    o = _dsa(u, p1, b, s)
    x3 = _mm_res([o], [w_out1.astype(MXU_DTYPE)], x2, tm=512, tn=512)
    x4 = _peer(x3, norm_ffn1, peer_wq1, peer_keys1, peer_u1, peer_v1)
    return _norm(x4, norm_final).reshape(b, s, d)
```

```python
import functools

import numpy as np
import jax
import jax.numpy as jnp
from jax import lax
from jax.experimental import pallas as pl
from jax.experimental.pallas import tpu as pltpu

F32 = jnp.float32
BF16 = jnp.bfloat16
MXU_DTYPE = jnp.bfloat16

LANES = 128
EPS = 1e-5
ROPE_THETA = 500000.0
BLOCK = 128
NEG_INF = -1e30
INT_MIN = -(2 ** 31)

SWA_WINDOW = 128
A_Q_HEADS, A_KV_HEADS, A_HEAD_DIM = 16, 4, 64
A_Q_COLS = A_Q_HEADS * A_HEAD_DIM
A_KV_COLS = A_KV_HEADS * A_HEAD_DIM
B_WIDTH = 1024
C_Q_RANK, C_HEADS, C_HEAD_DIM = 512, 16, 128
IDX_HEADS, IDX_HEAD_DIM, INDEX_TOPK = 16, 64, 256
IN1_COLS_PADDED = 896
PEER_HEADS, PEER_N_KEYS, PEER_TOPK, PEER_HALF = 8, 128, 16, 128
PEER_EXPERTS = PEER_N_KEYS * PEER_N_KEYS

VMEM_LIMIT = 56 * 1024 * 1024


def _cparams(*sem):
    return pltpu.CompilerParams(dimension_semantics=sem, vmem_limit_bytes=VMEM_LIMIT)


def _resident(n_tiles):
    return pl.Buffered(1) if n_tiles == 1 else None


def _col_tiles(w, tn):
    k, n = w.shape
    return w.reshape(k, n // tn, tn).transpose(1, 0, 2)


def _dot(a, b):
    return jnp.dot(a.astype(MXU_DTYPE), b.astype(MXU_DTYPE), preferred_element_type=F32)


def _dot_nt(a, b):
    return lax.dot_general(a.astype(MXU_DTYPE), b.astype(MXU_DTYPE),
                           (((1,), (1,)), ((), ())), preferred_element_type=F32)


def _rope_table(pos, heads, half):
    rot = 2 * half
    inv = ROPE_THETA ** (-jnp.arange(half, dtype=F32) * 2.0 / rot)
    freq = np.zeros(LANES, np.int32)
    rotated = np.zeros(LANES, bool)
    upper = np.zeros(LANES, bool)
    for lane0, dh in heads:
        assert dh // 8 == half
        freq[lane0:lane0 + rot] = np.arange(rot) % half
        rotated[lane0:lane0 + rot] = True
        upper[lane0 + half:lane0 + rot] = True
    inv_lane = jnp.where(rotated, inv[freq], 0.0)
    ang = pos.astype(F32)[:, None] * inv_lane[None, :]
    cos, sin = jnp.cos(ang), jnp.sin(ang)
    s1 = jnp.where(upper, sin, 0.0)
    s2 = jnp.where(rotated & ~upper, -sin, 0.0)
    return jnp.stack([cos, s1, s2])


def _rope_chunk(xc, tab_ref, half):
    return (xc * tab_ref[0] + pltpu.roll(xc, half, 1) * tab_ref[1]
            + pltpu.roll(xc, LANES - half, 1) * tab_ref[2])


def _proj_kernel(*refs, n_tabs, halves, groups, tn, with_xn):
    x_ref, g_ref, w_ref = refs[:3]
    tab_refs = refs[3:3 + n_tabs]
    o_ref = refs[3 + n_tabs]
    xn_out = refs[4 + n_tabs] if with_xn else None
    xn_ref = refs[-1]
    j = pl.program_id(1)

    @pl.when(j == 0)
    def _():
        x = x_ref[...]
        ms = jnp.mean(x * x, axis=-1, keepdims=True)
        xn = (x * lax.rsqrt(ms + EPS) * g_ref[...]).astype(MXU_DTYPE)
        xn_ref[...] = xn
        if with_xn:
            xn_out[...] = xn

    acc = jnp.dot(xn_ref[...], w_ref[...], preferred_element_type=F32)

    for (j_lo, j_hi), cfg in groups:
        @pl.when((j >= j_lo) & (j < j_hi))
        def _(cfg=cfg):
            for c in range(tn // LANES):
                xc = acc[:, c * LANES:(c + 1) * LANES]
                if cfg[c] >= 0:
                    xc = _rope_chunk(xc, tab_refs[cfg[c]], halves[cfg[c]])
                o_ref[:, c * LANES:(c + 1) * LANES] = xc


def _proj(x, g, w, *, k_cols, tm, tn, tabs=(), halves=(), groups=None, with_xn=False):
    t = x.shape[0]
    n = w.shape[1]
    nj = n // tn
    if groups is None:
        groups = (((0, nj), (-1,) * (tn // LANES)),)
    kern = functools.partial(_proj_kernel, n_tabs=len(tabs), halves=tuple(halves),
                             groups=tuple(groups), tn=tn, with_xn=with_xn)
    in_specs = [pl.BlockSpec((tm, k_cols), lambda i, j: (i, 0)),
                pl.BlockSpec((1, k_cols), lambda i, j: (0, 0)),
                pl.BlockSpec((None, k_cols, tn), lambda i, j: (j, 0, 0),
                             pipeline_mode=_resident(nj))]
    in_specs += [pl.BlockSpec((3, tm, LANES), lambda i, j: (0, i, 0)) for _ in tabs]
    out_shape = [jax.ShapeDtypeStruct((t, n), F32)]
    out_specs = [pl.BlockSpec((tm, tn), lambda i, j: (i, j))]
    if with_xn:
        out_shape.append(jax.ShapeDtypeStruct((t, k_cols), MXU_DTYPE))
        out_specs.append(pl.BlockSpec((tm, k_cols), lambda i, j: (i, 0)))
    res = pl.pallas_call(
        kern, grid=(t // tm, nj), in_specs=in_specs, out_specs=out_specs, out_shape=out_shape,
        scratch_shapes=[pltpu.VMEM((tm, k_cols), MXU_DTYPE)],
        compiler_params=_cparams("parallel", "arbitrary"), name="norm_proj",
    )(x, g.reshape(1, k_cols), _col_tiles(w, tn), *tabs)
    return res if with_xn else res[0]


def _mm_res_kernel(*refs, n_lhs):
    a_refs = refs[:n_lhs]
    w_refs = refs[n_lhs:2 * n_lhs]
    x_ref = refs[2 * n_lhs]
    o_ref = refs[2 * n_lhs + 1]
    ab_refs = refs[2 * n_lhs + 2:]

    @pl.when(pl.program_id(1) == 0)
    def _():
        for a_ref, ab_ref in zip(a_refs, ab_refs):
            ab_ref[...] = a_ref[...].astype(MXU_DTYPE)

    acc = x_ref[...]
    for ab_ref, w_ref in zip(ab_refs, w_refs):
        acc = acc + jnp.dot(ab_ref[...], w_ref[...], preferred_element_type=F32)
    o_ref[...] = acc


def _mm_res(lhs, ws, x, *, tm, tn):
    t, d = x.shape
    n_lhs = len(lhs)
    in_specs = [pl.BlockSpec((tm, a.shape[1]), lambda i, j: (i, 0)) for a in lhs]
    in_specs += [pl.BlockSpec((None, w.shape[0], tn), lambda i, j: (j, 0, 0),
                              pipeline_mode=_resident(d // tn)) for w in ws]
    in_specs += [pl.BlockSpec((tm, tn), lambda i, j: (i, j))]
    return pl.pallas_call(
        functools.partial(_mm_res_kernel, n_lhs=n_lhs), grid=(t // tm, d // tn),
        in_specs=in_specs, out_specs=pl.BlockSpec((tm, tn), lambda i, j: (i, j)),
        out_shape=jax.ShapeDtypeStruct((t, d), F32),
        scratch_shapes=[pltpu.VMEM((tm, a.shape[1]), MXU_DTYPE) for a in lhs],
        compiler_params=_cparams("parallel", "arbitrary"), name="out_proj",
    )(*lhs, *[_col_tiles(w, tn) for w in ws], x)


def _swa_kernel(sink_ref, q_ref, kp_ref, kc_ref, vp_ref, vc_ref, o_ref):
    n = pl.program_id(1)
    g_sz = A_Q_HEADS // A_KV_HEADS
    dh = A_HEAD_DIM
    q = q_ref[...]
    k2 = jnp.concatenate([kp_ref[...], kc_ref[...]], axis=0).astype(MXU_DTYPE)
    v2 = jnp.concatenate([vp_ref[...], vc_ref[...]], axis=0).astype(MXU_DTYPE)
    qi = lax.broadcasted_iota(jnp.int32, (BLOCK, 2 * BLOCK), 0)
    kj = lax.broadcasted_iota(jnp.int32, (BLOCK, 2 * BLOCK), 1)
    diff = qi + BLOCK - kj
    valid = (diff >= 0) & (diff < SWA_WINDOW) & ((kj >= BLOCK) | (n > 0))
    scale = dh ** -0.5
    for h in range(A_KV_HEADS):
        kh = k2[:, h * dh:(h + 1) * dh]
        vh = v2[:, h * dh:(h + 1) * dh]
        for g in range(g_sz):
            hq = h * g_sz + g
            s = _dot_nt(q[:, hq * dh:(hq + 1) * dh], kh) * scale
            s = jnp.where(valid, s, NEG_INF)
            sink = sink_ref[hq]
            m = jnp.maximum(jnp.max(s, axis=-1, keepdims=True), sink)
            p = jnp.exp(s - m)
            denom = jnp.sum(p, axis=-1, keepdims=True) + jnp.exp(sink - m)
            o_ref[:, hq * dh:(hq + 1) * dh] = _dot(p, vh) / denom


def _swa(p0, sinks, b, s):
    nb = s // BLOCK
    kv_w = A_KV_COLS
    k_blk = A_Q_COLS // kv_w
    v_blk = k_blk + 1
    cur = lambda bi, n: bi * nb + n
    prev = lambda bi, n: bi * nb + jnp.maximum(n - 1, 0)
    return pl.pallas_call(
        _swa_kernel, grid=(b, nb),
        in_specs=[pl.BlockSpec(memory_space=pltpu.SMEM),
                  pl.BlockSpec((BLOCK, A_Q_COLS), lambda bi, n: (cur(bi, n), 0)),
                  pl.BlockSpec((BLOCK, kv_w), lambda bi, n: (prev(bi, n), k_blk)),
                  pl.BlockSpec((BLOCK, kv_w), lambda bi, n: (cur(bi, n), k_blk)),
                  pl.BlockSpec((BLOCK, kv_w), lambda bi, n: (prev(bi, n), v_blk)),
                  pl.BlockSpec((BLOCK, kv_w), lambda bi, n: (cur(bi, n), v_blk))],
        out_specs=pl.BlockSpec((BLOCK, A_Q_COLS), lambda bi, n: (cur(bi, n), 0)),
        out_shape=jax.ShapeDtypeStruct((b * s, A_Q_COLS), F32),
        compiler_params=_cparams("parallel", "arbitrary"), name="swa",
    )(sinks, p0, p0, p0, p0, p0)


CONV_ROWS = 256
CONV_COLS = 512
HALO = 8


def _conv_kernel(bg_ref, cg_ref, hx_ref, cgh_ref, hxh_ref, w_ref, o_ref, *, tiles_per_seq):
    i = pl.program_id(0)
    z = cg_ref[...] * hx_ref[...]
    first = (i % tiles_per_seq) == 0
    zh = jnp.where(first, 0.0, cgh_ref[...] * hxh_ref[...])
    row = lax.broadcasted_iota(jnp.int32, z.shape, 0)
    z1 = pltpu.roll(z, 1, 0)
    z1 = jnp.where(row == 0, zh[HALO - 1:HALO], z1)
    z2 = pltpu.roll(z, 2, 0)
    z2 = jnp.where(row == 0, zh[HALO - 2:HALO - 1], jnp.where(row == 1, zh[HALO - 1:HALO], z2))
    w = w_ref[...]
    o_ref[...] = bg_ref[...] * (w[0:1] * z2 + w[1:2] * z1 + w[2:3] * z)


def _conv(p0, conv_w, s):
    t = p0.shape[0]
    col0 = A_Q_COLS + 2 * A_KV_COLS
    bg_blk, cg_blk, hx_blk = (col0 // CONV_COLS, (col0 + B_WIDTH) // CONV_COLS,
                              (col0 + 2 * B_WIDTH) // CONV_COLS)
    rh = CONV_ROWS // HALO
    halo_row = lambda i: jnp.maximum(i * rh - 1, 0)
    return pl.pallas_call(
        functools.partial(_conv_kernel, tiles_per_seq=s // CONV_ROWS),
        grid=(t // CONV_ROWS, B_WIDTH // CONV_COLS),
        in_specs=[pl.BlockSpec((CONV_ROWS, CONV_COLS), lambda i, j: (i, bg_blk + j)),
                  pl.BlockSpec((CONV_ROWS, CONV_COLS), lambda i, j: (i, cg_blk + j)),
                  pl.BlockSpec((CONV_ROWS, CONV_COLS), lambda i, j: (i, hx_blk + j)),
                  pl.BlockSpec((HALO, CONV_COLS), lambda i, j: (halo_row(i), cg_blk + j)),
                  pl.BlockSpec((HALO, CONV_COLS), lambda i, j: (halo_row(i), hx_blk + j)),
                  pl.BlockSpec((3, CONV_COLS), lambda i, j: (0, j))],
        out_specs=pl.BlockSpec((CONV_ROWS, CONV_COLS), lambda i, j: (i, j)),
        out_shape=jax.ShapeDtypeStruct((t, B_WIDTH), F32),
        compiler_params=_cparams("parallel", "parallel"), name="gated_conv",
    )(p0, p0, p0, p0, p0, conv_w)


DSA_WIDTHS = 4


def _dsa_block(qa_ref, qb_ref, iq_ref, iwq_ref, k_ref, v_ref, ikw_ref, o_ref, key_ref, half_ref,
               n, w, k_sel):
    ik = ikw_ref[0:w, 0:IDX_HEAD_DIM].astype(MXU_DTYPE)
    iw = iwq_ref[:, IDX_HEAD_DIM:IDX_HEAD_DIM + IDX_HEADS] * (
        (IDX_HEADS ** -0.5) * (IDX_HEAD_DIM ** -0.5))
    iq = iq_ref[...]
    score = jnp.zeros((BLOCK, w), F32)
    for h in range(IDX_HEADS):
        lg = _dot_nt(iq[:, h * IDX_HEAD_DIM:(h + 1) * IDX_HEAD_DIM], ik)
        score = score + iw[:, h:h + 1] * jnp.maximum(lg, 0.0)
    score = score + 0.0
    qpos = n * BLOCK + lax.broadcasted_iota(jnp.int32, (BLOCK, w), 0)
    kpos = lax.broadcasted_iota(jnp.int32, (BLOCK, w), 1)
    bits = pltpu.bitcast(score, jnp.int32)
    key = jnp.where(bits < 0, bits ^ jnp.int32(0x7FFFFFFF), bits)
    key = jnp.where(kpos <= qpos, key, INT_MIN)
    key_ref[:, 0:w] = key
    half_ref[:, 0:w] = jnp.right_shift(key, 16).astype(jnp.int16)

    def search16():
        def count_ge(t):
            t16 = t.astype(jnp.int16)
            acc = None
            for c in range(w // LANES):
                one = jnp.where(half_ref[:, c * LANES:(c + 1) * LANES] >= t16,
                                jnp.int16(1), jnp.int16(0))
                acc = one if acc is None else acc + one
            return jnp.sum(acc.astype(F32), axis=-1, keepdims=True)

        kf = float(k_sel)
        t0 = jnp.where(count_ge(jnp.zeros((BLOCK, 1), jnp.int32)) >= kf,
                       jnp.int32(0), jnp.int32(-2 ** 15))

        def bit_step(it, t):
            cand = t + jnp.left_shift(jnp.int32(1), jnp.int32(14) - it)
            return jnp.where(count_ge(cand) >= kf, cand, t)

        return lax.fori_loop(0, 15, bit_step, t0)

    t_hi = search16()
    hi = half_ref[:, 0:w].astype(jnp.int32)
    lo = jnp.bitwise_and(key_ref[:, 0:w], 0xFFFF) - 2 ** 15
    half_ref[:, 0:w] = jnp.where(hi > t_hi, 2 ** 15 - 1,
                                 jnp.where(hi < t_hi, -2 ** 15, lo)).astype(jnp.int16)
    t_lo = search16()
    thr = jnp.maximum(t_hi * 2 ** 16 + (t_lo + 2 ** 15), INT_MIN + 1)
    sel = key_ref[:, 0:w] >= thr

    kk = k_ref[0:w, :].astype(MXU_DTYPE)
    lane = lax.broadcasted_iota(jnp.int32, (w, LANES), 1)
    v_aug = jnp.concatenate([v_ref[0:w, :].astype(MXU_DTYPE),
                             jnp.where(lane == 0, 1.0, 0.0).astype(MXU_DTYPE)], axis=1)
    scale = C_HEAD_DIM ** -0.5
    for h in range(C_HEADS):
        q_r = qa_ref if h < C_HEADS // 2 else qb_ref
        hh = h % (C_HEADS // 2)
        s = _dot_nt(q_r[:, hh * C_HEAD_DIM:(hh + 1) * C_HEAD_DIM] * scale, kk)
        s = jnp.where(sel, s, NEG_INF)
        p = jnp.exp(s - jnp.max(s, axis=-1, keepdims=True))
        pv = _dot(p, v_aug)
        o_ref[:, h * C_HEAD_DIM:(h + 1) * C_HEAD_DIM] = (
            pv[:, :C_HEAD_DIM] / pv[:, C_HEAD_DIM:C_HEAD_DIM + 1])


def _dsa_kernel(qa_ref, qb_ref, iq_ref, iwq_ref, k_ref, v_ref, ikw_ref, o_ref, key_ref, half_ref,
                *, k_sel):
    n = pl.program_id(1)
    chunk = k_ref.shape[0] // DSA_WIDTHS
    n_chunks = (n * BLOCK + BLOCK - 1) // chunk + 1
    for wi in range(1, DSA_WIDTHS + 1):
        @pl.when(n_chunks == wi)
        def _(wi=wi):
            _dsa_block(qa_ref, qb_ref, iq_ref, iwq_ref, k_ref, v_ref, ikw_ref, o_ref, key_ref,
                       half_ref, n, wi * chunk, k_sel)


def _dsa(u, p1, b, s):
    nb = s // BLOCK
    k_sel = min(INDEX_TOPK, s // 4)
    half_q = C_HEADS * C_HEAD_DIM // 2
    k_blk = C_Q_RANK // LANES
    row = lambda bi, n: bi * nb + n
    return pl.pallas_call(
        functools.partial(_dsa_kernel, k_sel=k_sel), grid=(b, nb),
        in_specs=[pl.BlockSpec((BLOCK, half_q), lambda bi, n: (row(bi, n), 0)),
                  pl.BlockSpec((BLOCK, half_q), lambda bi, n: (row(bi, n), 1)),
                  pl.BlockSpec((BLOCK, half_q), lambda bi, n: (row(bi, n), 2)),
                  pl.BlockSpec((BLOCK, LANES), lambda bi, n: (row(bi, n), k_blk + 2)),
                  pl.BlockSpec((s, LANES), lambda bi, n: (bi, k_blk)),
                  pl.BlockSpec((s, LANES), lambda bi, n: (bi, k_blk + 1)),
                  pl.BlockSpec((s, LANES), lambda bi, n: (bi, k_blk + 2))],
        out_specs=pl.BlockSpec((BLOCK, C_HEADS * C_HEAD_DIM), lambda bi, n: (row(bi, n), 0)),
        out_shape=jax.ShapeDtypeStruct((b * s, C_HEADS * C_HEAD_DIM), F32),
        scratch_shapes=[pltpu.VMEM((BLOCK, s), jnp.int32), pltpu.VMEM((BLOCK, s), jnp.int16)],
        compiler_params=_cparams("parallel", "arbitrary"), name="dsa",
    )(u, u, u, p1, p1, p1, p1)


ROUTER_TOKENS = 256
N_FULL_K0 = 8
SUB = 8


def _top_rows(work, k, exact):
    rows, cols = work.shape
    row = lax.broadcasted_iota(jnp.int32, (rows, cols), 0).astype(F32)
    vrow = lax.broadcasted_iota(jnp.int32, (k, cols), 0)
    rank = jnp.full((rows, cols), float(k), F32)
    vals = jnp.zeros((k, cols), F32)
    for r in range(k):
        m = jnp.max(work, axis=0, keepdims=True)
        hit = work == m
        if exact:
            first = jnp.min(jnp.where(hit, row, float(rows)), axis=0, keepdims=True)
            hit = row == first
        rank = jnp.where(hit, float(r), rank)
        work = jnp.where(hit, -jnp.inf, work)
        vals = jnp.where(vrow == r, m, vals)
    return rank, vals


def _count_ranked(rank, k):
    return jnp.sum(jnp.where(rank < float(k), 1.0, 0.0), axis=0, keepdims=True)


def _pair_counts_exact(v0, v1, k):
    cand = jnp.concatenate([v0[k0:k0 + 1] + v1 for k0 in range(N_FULL_K0)]
                           + [v0[N_FULL_K0:] + v1[0:1]], axis=0)
    rank_c, _ = _top_rows(cand, k, True)
    sel = rank_c < float(k)
    cnt = jnp.where(sel, 1.0, 0.0)
    n_rank = jnp.concatenate(
        [jnp.sum(cnt[k0 * k:(k0 + 1) * k], axis=0, keepdims=True) for k0 in range(N_FULL_K0)]
        + [cnt[N_FULL_K0 * k:]], axis=0)
    z = jnp.sum(jnp.where(sel, jnp.exp(cand - cand[0:1]), 0.0), axis=0, keepdims=True)
    return n_rank, z, jnp.sum(cnt, axis=0, keepdims=True)


def _pair_counts_fast(v0, v1, k):
    cols = v0.shape[1]
    row = lax.broadcasted_iota(jnp.int32, (SUB, cols), 0)
    tiles = []
    for k1 in range(SUB):
        n_valid = min(k // (k1 + 1), SUB)
        tiles.append(jnp.where(row < n_valid, v0[0:SUB] + v1[k1:k1 + 1], -jnp.inf))
    tiles.append(v0[0:1] + v1[SUB:])
    tiles.append(v0[SUB:] + v1[0:1])
    cand = jnp.concatenate(tiles, axis=0)
    rank_c, _ = _top_rows(cand, k, False)
    sel = rank_c < float(k)
    cnt = jnp.where(sel, 1.0, 0.0)
    low = cnt[0:SUB]
    for k1 in range(1, SUB):
        low = low + cnt[k1 * SUB:(k1 + 1) * SUB]
    tail = jnp.sum(cnt[SUB * SUB:SUB * SUB + SUB], axis=0, keepdims=True)
    low = low + jnp.where(row == 0, tail, 0.0)
    n_rank = jnp.concatenate([low, cnt[SUB * SUB + SUB:]], axis=0)
    z = jnp.sum(jnp.where(sel, jnp.exp(cand - (v0[0:1] + v1[0:1])), 0.0), axis=0, keepdims=True)
    return n_rank, z, jnp.sum(cnt, axis=0, keepdims=True)


def _router_kernel(q_ref, keys_ref, p0_ref, n0_ref, p1_ref, r1_ref):
    k = PEER_TOPK

    def head(h, carry):
        col0 = pl.multiple_of(h * (2 * PEER_HALF), LANES)
        col1 = pl.multiple_of(col0 + PEER_HALF, LANES)
        s0 = _dot_nt(keys_ref[2 * h], q_ref[:, pl.ds(col0, PEER_HALF)])
        s1 = _dot_nt(keys_ref[2 * h + 1], q_ref[:, pl.ds(col1, PEER_HALF)])

        def route(exact):
            rank0, v0 = _top_rows(s0, k, exact)
            rank1, v1 = _top_rows(s1, k, exact)
            n_rank, z, n_sel = (_pair_counts_exact if exact else _pair_counts_fast)(v0, v1, k)
            n0 = jnp.zeros_like(s0)
            for r in range(k):
                n0 = jnp.where(rank0 == float(r), n_rank[r:r + 1], n0)
            p0_ref[h] = jnp.where(rank0 < float(k), jnp.exp(s0 - v0[0:1]), 0.0) / z
            n0_ref[h] = n0
            p1_ref[h] = jnp.where(rank1 < float(k), jnp.exp(s1 - v1[0:1]), 0.0).astype(p1_ref.dtype)
            r1_ref[h] = rank1.astype(r1_ref.dtype)
            return _count_ranked(rank0, k) + _count_ranked(rank1, k) + n_sel

        n_marked = route(False)

        @pl.when(jnp.max(n_marked) > 3.0 * k)
        def _():
            route(True)

        return carry

    lax.fori_loop(0, PEER_HEADS, head, 0)


def _router(q, keys):
    t = q.shape[0]
    tb = ROUTER_TOKENS
    blk = pl.BlockSpec((PEER_HEADS, PEER_N_KEYS, tb), lambda i: (0, 0, i))
    f32s = jax.ShapeDtypeStruct((PEER_HEADS, PEER_N_KEYS, t), F32)
    bf16s = jax.ShapeDtypeStruct((PEER_HEADS, PEER_N_KEYS, t), BF16)
    return pl.pallas_call(
        _router_kernel, grid=(t // tb,),
        in_specs=[pl.BlockSpec((tb, q.shape[1]), lambda i: (i, 0)),
                  pl.BlockSpec(keys.shape, lambda i: (0, 0, 0))],
        out_specs=[blk, blk, blk, blk], out_shape=[f32s, f32s, bf16s, bf16s],
        compiler_params=_cparams("parallel"), name="peer_router",
    )(q, keys)


EXPERT_TOKENS = 512
EXPERT_CHUNK = 1024
ROWS_PER_CHUNK = EXPERT_CHUNK // PEER_N_KEYS


PACK = 16


def _gate_tile(pre_ref, p0_ref, n0_ref, p1_ref, r1_ref, h_ref, tile, cache):
    tb = pre_ref.shape[1]
    ii, rt = divmod(tile, PEER_N_KEYS // PACK)
    if ii not in cache:
        cache.clear()
        cache[ii] = (
            [jnp.broadcast_to(p0_ref[h, ii:ii + 1, :], (PACK, tb)).astype(BF16)
             for h in range(PEER_HEADS)],
            [jnp.broadcast_to(n0_ref[h, ii:ii + 1, :], (PACK, tb)).astype(BF16)
             for h in range(PEER_HEADS)])
    gates, cnts = cache[ii]
    js = slice(rt * PACK, (rt + 1) * PACK)
    w = None
    for h in range(PEER_HEADS):
        term = jnp.where(r1_ref[h, js, :] < cnts[h], p1_ref[h, js, :],
                         jnp.zeros((), BF16)) * gates[h]
        w = term if w is None else w + term
    rows = slice(tile * PACK, (tile + 1) * PACK)
    pre = pre_ref[rows, :]
    act = 0.5 * pre * (1.0 + lax.erf(pre * np.sqrt(0.5).astype(np.float32)))
    h_ref[rows, :] = act.astype(BF16) * w


def _expert_kernel(xn_ref, x_ref, u_ref, vt_ref, p0_ref, n0_ref, p1_ref, r1_ref, o_ref, acc_ref,
                   pre_ref, h_ref):
    c = pl.program_id(1)

    @pl.when(c == 0)
    def _():
        acc_ref[...] = jnp.zeros_like(acc_ref)

    pre_ref[...] = _dot_nt(u_ref[...], xn_ref[...])
    cache = {}
    for tile in range(EXPERT_CHUNK // PACK):
        _gate_tile(pre_ref, p0_ref, n0_ref, p1_ref, r1_ref, h_ref, tile, cache)
    acc_ref[...] += jnp.dot(vt_ref[...], h_ref[...], preferred_element_type=F32)

    @pl.when(c == pl.num_programs(1) - 1)
    def _():
        o_ref[...] = x_ref[...] + acc_ref[...].T


def _experts(xn, x, u, vt, p0, n0, p1, r1):
    t, d = x.shape
    tb, ec = EXPERT_TOKENS, EXPERT_CHUNK
    small = pl.BlockSpec((PEER_HEADS, ROWS_PER_CHUNK, tb), lambda i, c: (0, c, i))
    full = pl.BlockSpec((PEER_HEADS, PEER_N_KEYS, tb), lambda i, c: (0, 0, i))
    return pl.pallas_call(
        _expert_kernel, grid=(t // tb, PEER_EXPERTS // ec),
        in_specs=[pl.BlockSpec((tb, d), lambda i, c: (i, 0)),
                  pl.BlockSpec((tb, d), lambda i, c: (i, 0)),
                  pl.BlockSpec((ec, d), lambda i, c: (c, 0)),
                  pl.BlockSpec((None, d, ec), lambda i, c: (c, 0, 0)),
                  small, small, full, full],
        out_specs=pl.BlockSpec((tb, d), lambda i, c: (i, 0)),
        out_shape=jax.ShapeDtypeStruct((t, d), F32),
        scratch_shapes=[pltpu.VMEM((d, tb), F32), pltpu.VMEM((ec, tb), F32),
                        pltpu.VMEM((ec, tb), BF16)],
        compiler_params=_cparams("parallel", "arbitrary"), name="peer_experts",
    )(xn, x, u, vt, p0, n0, p1, r1)


def _peer(x, g, wq, keys, u, v):
    xq, xn = _proj(x, g, wq.astype(MXU_DTYPE), k_cols=x.shape[1], tm=512, tn=wq.shape[1],
                   with_xn=True)
    keys2 = keys.reshape(2 * PEER_HEADS, PEER_N_KEYS, PEER_HALF).astype(MXU_DTYPE)
    p0, n0, p1, r1 = _router(xq, keys2)
    nc = PEER_EXPERTS // EXPERT_CHUNK
    vt = v.astype(MXU_DTYPE).reshape(nc, EXPERT_CHUNK, -1).transpose(0, 2, 1)
    return _experts(xn, x, u.astype(MXU_DTYPE), vt, p0, n0, p1, r1)


def _norm_kernel(x_ref, g_ref, o_ref):
    x = x_ref[...]
    ms = jnp.mean(x * x, axis=-1, keepdims=True)
    o_ref[...] = x * lax.rsqrt(ms + EPS) * g_ref[...]


def _norm(x, g, tm=512):
    t, d = x.shape
    return pl.pallas_call(
        _norm_kernel, grid=(t // tm,),
        in_specs=[pl.BlockSpec((tm, d), lambda i: (i, 0)), pl.BlockSpec((1, d), lambda i: (0, 0))],
        out_specs=pl.BlockSpec((tm, d), lambda i: (i, 0)),
        out_shape=jax.ShapeDtypeStruct((t, d), F32),
        compiler_params=_cparams("parallel"), name="final_norm",
    )(x, g.reshape(1, d))


def kernel(x, positions, norm_mix0, w_in0, sinks0, conv_w0, w_out0, norm_ffn0, peer_wq0, peer_keys0,
           peer_u0, peer_v0, norm_mix1, w_in1, g_qa1, w_uq1, w_out1, norm_ffn1, peer_wq1,
           peer_keys1, peer_u1, peer_v1, norm_final):
    b, s, d = x.shape
    t = b * s
    xf = x.reshape(t, d)
    pos = positions.reshape(t)
    tab_a = _rope_table(pos, ((0, 64), (64, 64)), 8)
    tab_b = _rope_table(pos, ((0, 128),), 16)
    tab_c = _rope_table(pos, ((0, 64),), 8)

    n0 = w_in0.shape[1]
    n_rope = (A_Q_COLS + A_KV_COLS) // LANES
    cfg0 = (0,) * n_rope + (-1,) * (n0 // LANES - n_rope)
    p0 = _proj(xf, norm_mix0, w_in0.astype(MXU_DTYPE), k_cols=d, tm=256, tn=n0,
               tabs=(tab_a,), halves=(8,), groups=(((0, 1), cfg0),))
    a_out = _swa(p0, sinks0, b, s)
    b_out = _conv(p0, conv_w0, s)
    w_o = w_out0.astype(MXU_DTYPE)
    x1 = _mm_res([a_out, b_out], [w_o[:A_Q_COLS], w_o[A_Q_COLS:]], xf, tm=512, tn=d)
    x2 = _peer(x1, norm_ffn0, peer_wq0, peer_keys0, peer_u0, peer_v0)

    w1 = jnp.pad(w_in1, ((0, 0), (0, IN1_COLS_PADDED - w_in1.shape[1]))).astype(MXU_DTYPE)
    k_chunk = C_Q_RANK // LANES
    cfg1 = tuple({k_chunk: 0, k_chunk + 2: 1}.get(c, -1) for c in range(IN1_COLS_PADDED // LANES))
    p1 = _proj(x2, norm_mix1, w1, k_cols=d, tm=512, tn=IN1_COLS_PADDED,
               tabs=(tab_b, tab_c), halves=(16, 8), groups=(((0, 1), cfg1),))
    n_q = C_HEADS * C_HEAD_DIM // LANES
    n_iq = IDX_HEADS * IDX_HEAD_DIM // LANES
    cfg_u = (0,) * n_q + (1,) * n_iq
    u = _proj(p1, g_qa1, w_uq1.astype(MXU_DTYPE), k_cols=C_Q_RANK, tm=256,
              tn=(n_q + n_iq) * LANES, tabs=(tab_b, tab_a), halves=(16, 8),
              groups=(((0, 1), cfg_u),))
    o = _dsa(u, p1, b, s)
    x3 = _mm_res([o], [w_out1.astype(MXU_DTYPE)], x2, tm=512, tn=d)
    x4 = _peer(x3, norm_ffn1, peer_wq1, peer_keys1, peer_u1, peer_v1)
    return _norm(x4, norm_final).reshape(b, s, d)
```

```python
import functools

import numpy as np
import jax
import jax.numpy as jnp
from jax import lax
from jax.experimental import pallas as pl
from jax.experimental.pallas import tpu as pltpu

F32 = jnp.float32
BF16 = jnp.bfloat16
MXU_DTYPE = jnp.bfloat16

LANES = 128
EPS = 1e-5
ROPE_THETA = 500000.0
BLOCK = 128
NEG_INF = -1e30
INT_MIN = -(2 ** 31)

SWA_WINDOW = 128
A_Q_HEADS, A_KV_HEADS, A_HEAD_DIM = 16, 4, 64
A_Q_COLS = A_Q_HEADS * A_HEAD_DIM
A_KV_COLS = A_KV_HEADS * A_HEAD_DIM
B_WIDTH = 1024
C_Q_RANK, C_HEADS, C_HEAD_DIM = 512, 16, 128
IDX_HEADS, IDX_HEAD_DIM, INDEX_TOPK = 16, 64, 256
IN1_COLS_PADDED = 896
PEER_HEADS, PEER_N_KEYS, PEER_TOPK, PEER_HALF = 8, 128, 16, 128
PEER_EXPERTS = PEER_N_KEYS * PEER_N_KEYS

VMEM_LIMIT = 56 * 1024 * 1024


def _cparams(*sem):
    return pltpu.CompilerParams(dimension_semantics=sem, vmem_limit_bytes=VMEM_LIMIT)


def _resident(n_tiles):
    return pl.Buffered(1) if n_tiles == 1 else None


def _col_tiles(w, tn):
    k, n = w.shape
    return w.reshape(k, n // tn, tn).transpose(1, 0, 2)


def _dot(a, b):
    return jnp.dot(a.astype(MXU_DTYPE), b.astype(MXU_DTYPE), preferred_element_type=F32)


def _dot_nt(a, b):
    return lax.dot_general(a.astype(MXU_DTYPE), b.astype(MXU_DTYPE),
                           (((1,), (1,)), ((), ())), preferred_element_type=F32)


def _rope_table(pos, heads, half):
    rot = 2 * half
    inv = ROPE_THETA ** (-jnp.arange(half, dtype=F32) * 2.0 / rot)
    freq = np.zeros(LANES, np.int32)
    rotated = np.zeros(LANES, bool)
    upper = np.zeros(LANES, bool)
    for lane0, dh in heads:
        assert dh // 8 == half
        freq[lane0:lane0 + rot] = np.arange(rot) % half
        rotated[lane0:lane0 + rot] = True
        upper[lane0 + half:lane0 + rot] = True
    inv_lane = jnp.where(rotated, inv[freq], 0.0)
    ang = pos.astype(F32)[:, None] * inv_lane[None, :]
    cos, sin = jnp.cos(ang), jnp.sin(ang)
    s1 = jnp.where(upper, sin, 0.0)
    s2 = jnp.where(rotated & ~upper, -sin, 0.0)
    return jnp.stack([cos, s1, s2])


def _rope_chunk(xc, tab_ref, half):
    return (xc * tab_ref[0] + pltpu.roll(xc, half, 1) * tab_ref[1]
            + pltpu.roll(xc, LANES - half, 1) * tab_ref[2])


def _proj_kernel(*refs, n_tabs, halves, groups, scales, tn, with_xn):
    x_ref, g_ref, w_ref = refs[:3]
    tab_refs = refs[3:3 + n_tabs]
    o_ref = refs[3 + n_tabs]
    xn_out = refs[4 + n_tabs] if with_xn else None
    xn_ref = refs[-1]
    j = pl.program_id(1)

    @pl.when(j == 0)
    def _():
        x = x_ref[...]
        ms = jnp.mean(x * x, axis=-1, keepdims=True)
        xn = (x * lax.rsqrt(ms + EPS) * g_ref[...]).astype(MXU_DTYPE)
        xn_ref[...] = xn
        if with_xn:
            xn_out[...] = xn

    acc = jnp.dot(xn_ref[...], w_ref[...], preferred_element_type=F32)

    for (j_lo, j_hi), cfg in groups:
        @pl.when((j >= j_lo) & (j < j_hi))
        def _(cfg=cfg):
            for c in range(tn // LANES):
                xc = acc[:, c * LANES:(c + 1) * LANES]
                if cfg[c] >= 0:
                    xc = _rope_chunk(xc, tab_refs[cfg[c]], halves[cfg[c]])
                if scales[c] != 1.0:
                    xc = xc * scales[c]
                o_ref[:, c * LANES:(c + 1) * LANES] = xc.astype(o_ref.dtype)


def _proj(x, g, w, *, k_cols, tm, tn, tabs=(), halves=(), groups=None, scales=None, with_xn=False,
          out_dtype=F32):
    t = x.shape[0]
    n = w.shape[1]
    nj = n // tn
    if groups is None:
        groups = (((0, nj), (-1,) * (tn // LANES)),)
    if scales is None:
        scales = (1.0,) * (tn // LANES)
    kern = functools.partial(_proj_kernel, n_tabs=len(tabs), halves=tuple(halves),
                             groups=tuple(groups), scales=tuple(scales), tn=tn, with_xn=with_xn)
    in_specs = [pl.BlockSpec((tm, k_cols), lambda i, j: (i, 0)),
                pl.BlockSpec((1, k_cols), lambda i, j: (0, 0)),
                pl.BlockSpec((None, k_cols, tn), lambda i, j: (j, 0, 0),
                             pipeline_mode=_resident(nj))]
    in_specs += [pl.BlockSpec((3, tm, LANES), lambda i, j: (0, i, 0)) for _ in tabs]
    out_shape = [jax.ShapeDtypeStruct((t, n), out_dtype)]
    out_specs = [pl.BlockSpec((tm, tn), lambda i, j: (i, j))]
    if with_xn:
        out_shape.append(jax.ShapeDtypeStruct((t, k_cols), MXU_DTYPE))
        out_specs.append(pl.BlockSpec((tm, k_cols), lambda i, j: (i, 0)))
    res = pl.pallas_call(
        kern, grid=(t // tm, nj), in_specs=in_specs, out_specs=out_specs, out_shape=out_shape,
        scratch_shapes=[pltpu.VMEM((tm, k_cols), MXU_DTYPE)],
        compiler_params=_cparams("parallel", "arbitrary"), name="norm_proj",
    )(x, g.reshape(1, k_cols), _col_tiles(w, tn), *tabs)
    return res if with_xn else res[0]


def _mm_res_kernel(*refs, n_lhs):
    a_refs = refs[:n_lhs]
    w_refs = refs[n_lhs:2 * n_lhs]
    x_ref = refs[2 * n_lhs]
    o_ref = refs[2 * n_lhs + 1]
    ab_refs = refs[2 * n_lhs + 2:]

    @pl.when(pl.program_id(1) == 0)
    def _():
        for a_ref, ab_ref in zip(a_refs, ab_refs):
            ab_ref[...] = a_ref[...].astype(MXU_DTYPE)

    acc = x_ref[...]
    for ab_ref, w_ref in zip(ab_refs, w_refs):
        acc = acc + jnp.dot(ab_ref[...], w_ref[...], preferred_element_type=F32)
    o_ref[...] = acc


def _mm_res(lhs, ws, x, *, tm, tn):
    t, d = x.shape
    n_lhs = len(lhs)
    in_specs = [pl.BlockSpec((tm, a.shape[1]), lambda i, j: (i, 0)) for a in lhs]
    in_specs += [pl.BlockSpec((None, w.shape[0], tn), lambda i, j: (j, 0, 0),
                              pipeline_mode=_resident(d // tn)) for w in ws]
    in_specs += [pl.BlockSpec((tm, tn), lambda i, j: (i, j))]
    return pl.pallas_call(
        functools.partial(_mm_res_kernel, n_lhs=n_lhs), grid=(t // tm, d // tn),
        in_specs=in_specs, out_specs=pl.BlockSpec((tm, tn), lambda i, j: (i, j)),
        out_shape=jax.ShapeDtypeStruct((t, d), F32),
        scratch_shapes=[pltpu.VMEM((tm, a.shape[1]), MXU_DTYPE) for a in lhs],
        compiler_params=_cparams("parallel", "arbitrary"), name="out_proj",
    )(*lhs, *[_col_tiles(w, tn) for w in ws], x)


def _swa_kernel(sink_ref, q_ref, kp_ref, kc_ref, vp_ref, vc_ref, o_ref):
    n = pl.program_id(1)
    g_sz = A_Q_HEADS // A_KV_HEADS
    dh = A_HEAD_DIM
    q = q_ref[...]
    k2 = jnp.concatenate([kp_ref[...], kc_ref[...]], axis=0).astype(MXU_DTYPE)
    v2 = jnp.concatenate([vp_ref[...], vc_ref[...]], axis=0).astype(MXU_DTYPE)
    qi = lax.broadcasted_iota(jnp.int32, (BLOCK, 2 * BLOCK), 0)
    kj = lax.broadcasted_iota(jnp.int32, (BLOCK, 2 * BLOCK), 1)
    diff = qi + BLOCK - kj
    valid = (diff >= 0) & (diff < SWA_WINDOW) & ((kj >= BLOCK) | (n > 0))
    scale = dh ** -0.5
    for h in range(A_KV_HEADS):
        kh = k2[:, h * dh:(h + 1) * dh]
        vh = v2[:, h * dh:(h + 1) * dh]
        for g in range(g_sz):
            hq = h * g_sz + g
            s = _dot_nt(q[:, hq * dh:(hq + 1) * dh], kh) * scale
            s = jnp.where(valid, s, NEG_INF)
            sink = sink_ref[hq]
            m = jnp.maximum(jnp.max(s, axis=-1, keepdims=True), sink)
            p = jnp.exp(s - m)
            denom = jnp.sum(p, axis=-1, keepdims=True) + jnp.exp(sink - m)
            o_ref[:, hq * dh:(hq + 1) * dh] = (_dot(p, vh) / denom).astype(o_ref.dtype)


def _swa(p0, sinks, b, s):
    nb = s // BLOCK
    kv_w = A_KV_COLS
    k_blk = A_Q_COLS // kv_w
    v_blk = k_blk + 1
    cur = lambda bi, n: bi * nb + n
    prev = lambda bi, n: bi * nb + jnp.maximum(n - 1, 0)
    return pl.pallas_call(
        _swa_kernel, grid=(b, nb),
        in_specs=[pl.BlockSpec(memory_space=pltpu.SMEM),
                  pl.BlockSpec((BLOCK, A_Q_COLS), lambda bi, n: (cur(bi, n), 0)),
                  pl.BlockSpec((BLOCK, kv_w), lambda bi, n: (prev(bi, n), k_blk)),
                  pl.BlockSpec((BLOCK, kv_w), lambda bi, n: (cur(bi, n), k_blk)),
                  pl.BlockSpec((BLOCK, kv_w), lambda bi, n: (prev(bi, n), v_blk)),
                  pl.BlockSpec((BLOCK, kv_w), lambda bi, n: (cur(bi, n), v_blk))],
        out_specs=pl.BlockSpec((BLOCK, A_Q_COLS), lambda bi, n: (cur(bi, n), 0)),
        out_shape=jax.ShapeDtypeStruct((b * s, A_Q_COLS), MXU_DTYPE),
        compiler_params=_cparams("parallel", "arbitrary"), name="swa",
    )(sinks, p0, p0, p0, p0, p0)


CONV_ROWS = 256
CONV_COLS = 512
HALO = 8


def _conv_kernel(bg_ref, cg_ref, hx_ref, cgh_ref, hxh_ref, w_ref, o_ref, *, tiles_per_seq):
    i = pl.program_id(0)
    z = cg_ref[...] * hx_ref[...]
    first = (i % tiles_per_seq) == 0
    zh = jnp.where(first, 0.0, cgh_ref[...] * hxh_ref[...])
    row = lax.broadcasted_iota(jnp.int32, z.shape, 0)
    z1 = pltpu.roll(z, 1, 0)
    z1 = jnp.where(row == 0, zh[HALO - 1:HALO], z1)
    z2 = pltpu.roll(z, 2, 0)
    z2 = jnp.where(row == 0, zh[HALO - 2:HALO - 1], jnp.where(row == 1, zh[HALO - 1:HALO], z2))
    w = w_ref[...]
    o_ref[...] = (bg_ref[...] * (w[0:1] * z2 + w[1:2] * z1 + w[2:3] * z)).astype(o_ref.dtype)


def _conv(p0, conv_w, s):
    t = p0.shape[0]
    col0 = A_Q_COLS + 2 * A_KV_COLS
    bg_blk, cg_blk, hx_blk = (col0 // CONV_COLS, (col0 + B_WIDTH) // CONV_COLS,
                              (col0 + 2 * B_WIDTH) // CONV_COLS)
    rh = CONV_ROWS // HALO
    halo_row = lambda i: jnp.maximum(i * rh - 1, 0)
    return pl.pallas_call(
        functools.partial(_conv_kernel, tiles_per_seq=s // CONV_ROWS),
        grid=(t // CONV_ROWS, B_WIDTH // CONV_COLS),
        in_specs=[pl.BlockSpec((CONV_ROWS, CONV_COLS), lambda i, j: (i, bg_blk + j)),
                  pl.BlockSpec((CONV_ROWS, CONV_COLS), lambda i, j: (i, cg_blk + j)),
                  pl.BlockSpec((CONV_ROWS, CONV_COLS), lambda i, j: (i, hx_blk + j)),
                  pl.BlockSpec((HALO, CONV_COLS), lambda i, j: (halo_row(i), cg_blk + j)),
                  pl.BlockSpec((HALO, CONV_COLS), lambda i, j: (halo_row(i), hx_blk + j)),
                  pl.BlockSpec((3, CONV_COLS), lambda i, j: (0, j))],
        out_specs=pl.BlockSpec((CONV_ROWS, CONV_COLS), lambda i, j: (i, j)),
        out_shape=jax.ShapeDtypeStruct((t, B_WIDTH), MXU_DTYPE),
        compiler_params=_cparams("parallel", "parallel"), name="gated_conv",
    )(p0, p0, p0, p0, p0, conv_w)


DSA_WIDTHS = 8


def _dsa_block(qa_ref, qb_ref, iq_ref, iwq_ref, k_ref, v_ref, ikw_ref, o_ref, key_ref, half_ref,
               n, w, k_sel):
    ik = ikw_ref[0:w, 0:IDX_HEAD_DIM].astype(MXU_DTYPE)
    iw = iwq_ref[:, IDX_HEAD_DIM:IDX_HEAD_DIM + IDX_HEADS] * (
        (IDX_HEADS ** -0.5) * (IDX_HEAD_DIM ** -0.5))
    iq = iq_ref[...]
    score = jnp.zeros((BLOCK, w), F32)
    for h in range(IDX_HEADS):
        lg = _dot_nt(iq[:, h * IDX_HEAD_DIM:(h + 1) * IDX_HEAD_DIM], ik)
        score = score + iw[:, h:h + 1] * jnp.maximum(lg, 0.0)
    score = score + 0.0
    qpos = n * BLOCK + lax.broadcasted_iota(jnp.int32, (BLOCK, w), 0)
    kpos = lax.broadcasted_iota(jnp.int32, (BLOCK, w), 1)
    bits = pltpu.bitcast(score, jnp.int32)
    key = jnp.where(bits < 0, bits ^ jnp.int32(0x7FFFFFFF), bits)
    key = jnp.where(kpos <= qpos, key, INT_MIN)
    key_ref[:, 0:w] = key
    half_ref[:, 0:w] = jnp.right_shift(key, 16).astype(jnp.int16)

    def search16():
        def count_ge(t):
            t16 = t.astype(jnp.int16)
            acc = None
            for c in range(w // LANES):
                one = jnp.where(half_ref[:, c * LANES:(c + 1) * LANES] >= t16,
                                jnp.int16(1), jnp.int16(0))
                acc = one if acc is None else acc + one
            return jnp.sum(acc.astype(F32), axis=-1, keepdims=True)

        kf = float(k_sel)
        t0 = jnp.where(count_ge(jnp.zeros((BLOCK, 1), jnp.int32)) >= kf,
                       jnp.int32(0), jnp.int32(-2 ** 15))

        def bit_step(it, t):
            cand = t + jnp.left_shift(jnp.int32(1), jnp.int32(14) - it)
            return jnp.where(count_ge(cand) >= kf, cand, t)

        return lax.fori_loop(0, 15, bit_step, t0)

    t_hi = search16()
    hi = half_ref[:, 0:w].astype(jnp.int32)
    lo = jnp.bitwise_and(key_ref[:, 0:w], 0xFFFF) - 2 ** 15
    half_ref[:, 0:w] = jnp.where(hi > t_hi, 2 ** 15 - 1,
                                 jnp.where(hi < t_hi, -2 ** 15, lo)).astype(jnp.int16)
    t_lo = search16()
    thr = jnp.maximum(t_hi * 2 ** 16 + (t_lo + 2 ** 15), INT_MIN + 1)
    sel = key_ref[:, 0:w] >= thr

    kk = k_ref[0:w, :].astype(MXU_DTYPE)
    lane = lax.broadcasted_iota(jnp.int32, (w, LANES), 1)
    v_aug = jnp.concatenate([v_ref[0:w, :].astype(MXU_DTYPE),
                             jnp.where(lane == 0, 1.0, 0.0).astype(MXU_DTYPE)], axis=1)
    for h in range(C_HEADS):
        q_r = qa_ref if h < C_HEADS // 2 else qb_ref
        hh = h % (C_HEADS // 2)
        s = _dot_nt(q_r[:, hh * C_HEAD_DIM:(hh + 1) * C_HEAD_DIM], kk)
        s = jnp.where(sel, s, NEG_INF)
        p = jnp.exp(s - jnp.max(s, axis=-1, keepdims=True))
        pv = _dot(p, v_aug)
        o_ref[:, h * C_HEAD_DIM:(h + 1) * C_HEAD_DIM] = (
            pv[:, :C_HEAD_DIM] / pv[:, C_HEAD_DIM:C_HEAD_DIM + 1]).astype(o_ref.dtype)


def _dsa_kernel(qa_ref, qb_ref, iq_ref, iwq_ref, k_ref, v_ref, ikw_ref, o_ref, key_ref, half_ref,
                *, k_sel):
    n = pl.program_id(1)
    chunk = k_ref.shape[0] // DSA_WIDTHS
    n_chunks = (n * BLOCK + BLOCK - 1) // chunk + 1
    for wi in range(1, DSA_WIDTHS + 1):
        if (wi * chunk) % BLOCK:
            continue

        @pl.when(n_chunks == wi)
        def _(wi=wi):
            _dsa_block(qa_ref, qb_ref, iq_ref, iwq_ref, k_ref, v_ref, ikw_ref, o_ref, key_ref,
                       half_ref, n, wi * chunk, k_sel)


def _dsa(u, p1, b, s):
    nb = s // BLOCK
    k_sel = min(INDEX_TOPK, s // 4)
    half_q = C_HEADS * C_HEAD_DIM // 2
    k_blk = C_Q_RANK // LANES
    row = lambda bi, n: bi * nb + n
    return pl.pallas_call(
        functools.partial(_dsa_kernel, k_sel=k_sel), grid=(b, nb),
        in_specs=[pl.BlockSpec((BLOCK, half_q), lambda bi, n: (row(bi, n), 0)),
                  pl.BlockSpec((BLOCK, half_q), lambda bi, n: (row(bi, n), 1)),
                  pl.BlockSpec((BLOCK, half_q), lambda bi, n: (row(bi, n), 2)),
                  pl.BlockSpec((BLOCK, LANES), lambda bi, n: (row(bi, n), k_blk + 2)),
                  pl.BlockSpec((s, LANES), lambda bi, n: (bi, k_blk)),
                  pl.BlockSpec((s, LANES), lambda bi, n: (bi, k_blk + 1)),
                  pl.BlockSpec((s, LANES), lambda bi, n: (bi, k_blk + 2))],
        out_specs=pl.BlockSpec((BLOCK, C_HEADS * C_HEAD_DIM), lambda bi, n: (row(bi, n), 0)),
        out_shape=jax.ShapeDtypeStruct((b * s, C_HEADS * C_HEAD_DIM), MXU_DTYPE),
        scratch_shapes=[pltpu.VMEM((BLOCK, s), jnp.int32), pltpu.VMEM((BLOCK, s), jnp.int16)],
        compiler_params=_cparams("parallel", "arbitrary"), name="dsa",
    )(u, u, u, p1, p1, p1, p1)


ROUTER_TOKENS = 256
N_FULL_K0 = 8
SUB = 8


def _top_rows(work, k, exact):
    rows, cols = work.shape
    row = lax.broadcasted_iota(jnp.int32, (rows, cols), 0).astype(F32)
    vrow = lax.broadcasted_iota(jnp.int32, (k, cols), 0)
    rank = jnp.full((rows, cols), float(k), F32)
    vals = jnp.zeros((k, cols), F32)
    for r in range(k):
        m = jnp.max(work, axis=0, keepdims=True)
        hit = work == m
        if exact:
            first = jnp.min(jnp.where(hit, row, float(rows)), axis=0, keepdims=True)
            hit = row == first
        rank = jnp.where(hit, float(r), rank)
        work = jnp.where(hit, -jnp.inf, work)
        vals = jnp.where(vrow == r, m, vals)
    return rank, vals


def _count_ranked(rank, k):
    return jnp.sum(jnp.where(rank < float(k), 1.0, 0.0), axis=0, keepdims=True)


def _pair_counts_exact(v0, v1, k):
    cand = jnp.concatenate([v0[k0:k0 + 1] + v1 for k0 in range(N_FULL_K0)]
                           + [v0[N_FULL_K0:] + v1[0:1]], axis=0)
    rank_c, _ = _top_rows(cand, k, True)
    sel = rank_c < float(k)
    cnt = jnp.where(sel, 1.0, 0.0)
    n_rank = jnp.concatenate(
        [jnp.sum(cnt[k0 * k:(k0 + 1) * k], axis=0, keepdims=True) for k0 in range(N_FULL_K0)]
        + [cnt[N_FULL_K0 * k:]], axis=0)
    z = jnp.sum(jnp.where(sel, jnp.exp(cand - cand[0:1]), 0.0), axis=0, keepdims=True)
    return n_rank, z, jnp.sum(cnt, axis=0, keepdims=True)


def _pair_counts_fast(v0, v1, k):
    cols = v0.shape[1]
    row = lax.broadcasted_iota(jnp.int32, (SUB, cols), 0)
    tiles = []
    for k1 in range(SUB):
        n_valid = min(k // (k1 + 1), SUB)
        tiles.append(jnp.where(row < n_valid, v0[0:SUB] + v1[k1:k1 + 1], -jnp.inf))
    tiles.append(v0[0:1] + v1[SUB:])
    tiles.append(v0[SUB:] + v1[0:1])
    cand = jnp.concatenate(tiles, axis=0)
    rank_c, _ = _top_rows(cand, k, False)
    sel = rank_c < float(k)
    cnt = jnp.where(sel, 1.0, 0.0)
    low = cnt[0:SUB]
    for k1 in range(1, SUB):
        low = low + cnt[k1 * SUB:(k1 + 1) * SUB]
    tail = jnp.sum(cnt[SUB * SUB:SUB * SUB + SUB], axis=0, keepdims=True)
    low = low + jnp.where(row == 0, tail, 0.0)
    n_rank = jnp.concatenate([low, cnt[SUB * SUB + SUB:]], axis=0)
    z = jnp.sum(jnp.where(sel, jnp.exp(cand - (v0[0:1] + v1[0:1])), 0.0), axis=0, keepdims=True)
    return n_rank, z, jnp.sum(cnt, axis=0, keepdims=True)


def _router_kernel(q_ref, keys_ref, p0_ref, n0_ref, p1_ref, r1_ref):
    k = PEER_TOPK

    def head(h, carry):
        col0 = pl.multiple_of(h * (2 * PEER_HALF), LANES)
        col1 = pl.multiple_of(col0 + PEER_HALF, LANES)
        s0 = _dot_nt(keys_ref[2 * h], q_ref[:, pl.ds(col0, PEER_HALF)])
        s1 = _dot_nt(keys_ref[2 * h + 1], q_ref[:, pl.ds(col1, PEER_HALF)])

        def route(exact):
            rank0, v0 = _top_rows(s0, k, exact)
            rank1, v1 = _top_rows(s1, k, exact)
            n_rank, z, n_sel = (_pair_counts_exact if exact else _pair_counts_fast)(v0, v1, k)
            n0 = jnp.zeros_like(s0)
            for r in range(k):
                n0 = jnp.where(rank0 == float(r), n_rank[r:r + 1], n0)
            p0_ref[h] = jnp.where(rank0 < float(k), jnp.exp(s0 - v0[0:1]), 0.0) / z
            n0_ref[h] = n0
            p1_ref[h] = jnp.where(rank1 < float(k), jnp.exp(s1 - v1[0:1]), 0.0).astype(p1_ref.dtype)
            r1_ref[h] = rank1.astype(r1_ref.dtype)
            return _count_ranked(rank0, k) + _count_ranked(rank1, k) + n_sel

        n_marked = route(False)

        @pl.when(jnp.max(n_marked) > 3.0 * k)
        def _():
            route(True)

        return carry

    lax.fori_loop(0, PEER_HEADS, head, 0)


def _router(q, keys):
    t = q.shape[0]
    tb = ROUTER_TOKENS
    blk = pl.BlockSpec((PEER_HEADS, PEER_N_KEYS, tb), lambda i: (0, 0, i))
    f32s = jax.ShapeDtypeStruct((PEER_HEADS, PEER_N_KEYS, t), F32)
    bf16s = jax.ShapeDtypeStruct((PEER_HEADS, PEER_N_KEYS, t), BF16)
    return pl.pallas_call(
        _router_kernel, grid=(t // tb,),
        in_specs=[pl.BlockSpec((tb, q.shape[1]), lambda i: (i, 0)),
                  pl.BlockSpec(keys.shape, lambda i: (0, 0, 0))],
        out_specs=[blk, blk, blk, blk], out_shape=[f32s, f32s, bf16s, bf16s],
        compiler_params=_cparams("parallel"), name="peer_router",
    )(q, keys)


EXPERT_TOKENS = 512
EXPERT_CHUNK = 1024
ROWS_PER_CHUNK = EXPERT_CHUNK // PEER_N_KEYS


MM1_SLICES = 2
PACK = 16


def _gate_tile(pre_ref, p0_ref, n0_ref, p1_ref, r1_ref, h_ref, tile, cache):
    tb = pre_ref.shape[1]
    ii, rt = divmod(tile, PEER_N_KEYS // PACK)
    if ii not in cache:
        cache.clear()
        cache[ii] = (
            [jnp.broadcast_to(p0_ref[h, ii:ii + 1, :], (PACK, tb)).astype(BF16)
             for h in range(PEER_HEADS)],
            [jnp.broadcast_to(n0_ref[h, ii:ii + 1, :], (PACK, tb)).astype(BF16)
             for h in range(PEER_HEADS)])
    gates, cnts = cache[ii]
    js = slice(rt * PACK, (rt + 1) * PACK)
    w = None
    for h in range(PEER_HEADS):
        term = jnp.where(r1_ref[h, js, :] < cnts[h], p1_ref[h, js, :],
                         jnp.zeros((), BF16)) * gates[h]
        w = term if w is None else w + term
    rows = slice(tile * PACK, (tile + 1) * PACK)
    pre = pre_ref[rows, :]
    act = 0.5 * pre * (1.0 + lax.erf(pre * np.sqrt(0.5).astype(np.float32)))
    h_ref[rows, :] = act.astype(BF16) * w


def _expert_kernel(xn_ref, x_ref, u_ref, vt_ref, p0_ref, n0_ref, p1_ref, r1_ref, o_ref, acc_ref,
                   pre_ref, h_ref):
    c = pl.program_id(1)

    @pl.when(c == 0)
    def _():
        acc_ref[...] = jnp.zeros_like(acc_ref)

    cache = {}
    n_tiles = EXPERT_CHUNK // PACK
    for k in range(MM1_SLICES):
        rows = slice(k * EXPERT_CHUNK // MM1_SLICES, (k + 1) * EXPERT_CHUNK // MM1_SLICES)
        pre_ref[rows, :] = _dot_nt(u_ref[rows, :], xn_ref[...])
        for tile in range(k * n_tiles // MM1_SLICES, (k + 1) * n_tiles // MM1_SLICES):
            _gate_tile(pre_ref, p0_ref, n0_ref, p1_ref, r1_ref, h_ref, tile, cache)
    acc_ref[...] += jnp.dot(vt_ref[...], h_ref[...], preferred_element_type=F32)

    @pl.when(c == pl.num_programs(1) - 1)
    def _():
        o_ref[...] = x_ref[...] + acc_ref[...].T


def _experts(xn, x, u, vt, p0, n0, p1, r1):
    t, d = x.shape
    tb, ec = EXPERT_TOKENS, EXPERT_CHUNK
    small = pl.BlockSpec((PEER_HEADS, ROWS_PER_CHUNK, tb), lambda i, c: (0, c, i))
    full = pl.BlockSpec((PEER_HEADS, PEER_N_KEYS, tb), lambda i, c: (0, 0, i))
    return pl.pallas_call(
        _expert_kernel, grid=(t // tb, PEER_EXPERTS // ec),
        in_specs=[pl.BlockSpec((tb, d), lambda i, c: (i, 0)),
                  pl.BlockSpec((tb, d), lambda i, c: (i, 0)),
                  pl.BlockSpec((ec, d), lambda i, c: (c, 0)),
                  pl.BlockSpec((None, d, ec), lambda i, c: (c, 0, 0)),
                  small, small, full, full],
        out_specs=pl.BlockSpec((tb, d), lambda i, c: (i, 0)),
        out_shape=jax.ShapeDtypeStruct((t, d), F32),
        scratch_shapes=[pltpu.VMEM((d, tb), F32), pltpu.VMEM((ec, tb), F32),
                        pltpu.VMEM((ec, tb), BF16)],
        compiler_params=_cparams("parallel", "arbitrary"), name="peer_experts",
    )(xn, x, u, vt, p0, n0, p1, r1)


def _peer(x, g, wq, keys, u, v):
    xq, xn = _proj(x, g, wq.astype(MXU_DTYPE), k_cols=x.shape[1], tm=512, tn=wq.shape[1],
                   with_xn=True, out_dtype=MXU_DTYPE)
    keys2 = keys.reshape(2 * PEER_HEADS, PEER_N_KEYS, PEER_HALF).astype(MXU_DTYPE)
    p0, n0, p1, r1 = _router(xq, keys2)
    nc = PEER_EXPERTS // EXPERT_CHUNK
    vt = v.astype(MXU_DTYPE).reshape(nc, EXPERT_CHUNK, -1).transpose(0, 2, 1)
    return _experts(xn, x, u.astype(MXU_DTYPE), vt, p0, n0, p1, r1)


def _norm_kernel(x_ref, g_ref, o_ref):
    x = x_ref[...]
    ms = jnp.mean(x * x, axis=-1, keepdims=True)
    o_ref[...] = x * lax.rsqrt(ms + EPS) * g_ref[...]


def _norm(x, g, tm=512):
    t, d = x.shape
    return pl.pallas_call(
        _norm_kernel, grid=(t // tm,),
        in_specs=[pl.BlockSpec((tm, d), lambda i: (i, 0)), pl.BlockSpec((1, d), lambda i: (0, 0))],
        out_specs=pl.BlockSpec((tm, d), lambda i: (i, 0)),
        out_shape=jax.ShapeDtypeStruct((t, d), F32),
        compiler_params=_cparams("parallel"), name="final_norm",
    )(x, g.reshape(1, d))


def kernel(x, positions, norm_mix0, w_in0, sinks0, conv_w0, w_out0, norm_ffn0, peer_wq0, peer_keys0,
           peer_u0, peer_v0, norm_mix1, w_in1, g_qa1, w_uq1, w_out1, norm_ffn1, peer_wq1,
           peer_keys1, peer_u1, peer_v1, norm_final):
    b, s, d = x.shape
    t = b * s
    xf = x.reshape(t, d)
    pos = positions.reshape(t)
    tab_a = _rope_table(pos, ((0, 64), (64, 64)), 8)
    tab_b = _rope_table(pos, ((0, 128),), 16)
    tab_c = _rope_table(pos, ((0, 64),), 8)

    n0 = w_in0.shape[1]
    n_rope = (A_Q_COLS + A_KV_COLS) // LANES
    cfg0 = (0,) * n_rope + (-1,) * (n0 // LANES - n_rope)
    p0 = _proj(xf, norm_mix0, w_in0.astype(MXU_DTYPE), k_cols=d, tm=256, tn=n0,
               tabs=(tab_a,), halves=(8,), groups=(((0, 1), cfg0),))
    a_out = _swa(p0, sinks0, b, s)
    b_out = _conv(p0, conv_w0, s)
    w_o = w_out0.astype(MXU_DTYPE)
    x1 = _mm_res([a_out, b_out], [w_o[:A_Q_COLS], w_o[A_Q_COLS:]], xf, tm=512, tn=d)
    x2 = _peer(x1, norm_ffn0, peer_wq0, peer_keys0, peer_u0, peer_v0)

    w1 = jnp.pad(w_in1, ((0, 0), (0, IN1_COLS_PADDED - w_in1.shape[1]))).astype(MXU_DTYPE)
    k_chunk = C_Q_RANK // LANES
    cfg1 = tuple({k_chunk: 0, k_chunk + 2: 1}.get(c, -1) for c in range(IN1_COLS_PADDED // LANES))
    p1 = _proj(x2, norm_mix1, w1, k_cols=d, tm=512, tn=IN1_COLS_PADDED,
               tabs=(tab_b, tab_c), halves=(16, 8), groups=(((0, 1), cfg1),))
    n_q = C_HEADS * C_HEAD_DIM // LANES
    n_iq = IDX_HEADS * IDX_HEAD_DIM // LANES
    cfg_u = (0,) * n_q + (1,) * n_iq
    u = _proj(p1, g_qa1, w_uq1.astype(MXU_DTYPE), k_cols=C_Q_RANK, tm=256,
              tn=(n_q + n_iq) * LANES, tabs=(tab_b, tab_a), halves=(16, 8),
              groups=(((0, 1), cfg_u),), scales=(C_HEAD_DIM ** -0.5,) * n_q + (1.0,) * n_iq,
              out_dtype=MXU_DTYPE)
    o = _dsa(u, p1, b, s)
    x3 = _mm_res([o], [w_out1.astype(MXU_DTYPE)], x2, tm=512, tn=d)
    x4 = _peer(x3, norm_ffn1, peer_wq1, peer_keys1, peer_u1, peer_v1)
    return _norm(x4, norm_final).reshape(b, s, d)
```

```python
import functools

import numpy as np
import jax
import jax.numpy as jnp
from jax import lax
from jax.experimental import pallas as pl
from jax.experimental.pallas import tpu as pltpu

F32 = jnp.float32
BF16 = jnp.bfloat16
MXU_DTYPE = jnp.bfloat16

LANES = 128
EPS = 1e-5
ROPE_THETA = 500000.0
BLOCK = 128
NEG_INF = -1e30
INT_MIN = -(2 ** 31)

SWA_WINDOW = 128
A_Q_HEADS, A_KV_HEADS, A_HEAD_DIM = 16, 4, 64
A_Q_COLS = A_Q_HEADS * A_HEAD_DIM
A_KV_COLS = A_KV_HEADS * A_HEAD_DIM
B_WIDTH = 1024
C_Q_RANK, C_HEADS, C_HEAD_DIM = 512, 16, 128
IDX_HEADS, IDX_HEAD_DIM, INDEX_TOPK = 16, 64, 256
IN1_COLS_PADDED = 896
PEER_HEADS, PEER_N_KEYS, PEER_TOPK, PEER_HALF = 8, 128, 16, 128
PEER_EXPERTS = PEER_N_KEYS * PEER_N_KEYS

VMEM_LIMIT = 56 * 1024 * 1024


def _cparams(*sem):
    return pltpu.CompilerParams(dimension_semantics=sem, vmem_limit_bytes=VMEM_LIMIT)


def _resident(n_tiles):
    return pl.Buffered(1) if n_tiles == 1 else None


def _col_tiles(w, tn):
    k, n = w.shape
    return w.reshape(k, n // tn, tn).transpose(1, 0, 2)


def _dot(a, b):
    return jnp.dot(a.astype(MXU_DTYPE), b.astype(MXU_DTYPE), preferred_element_type=F32)


def _dot_nt(a, b):
    return lax.dot_general(a.astype(MXU_DTYPE), b.astype(MXU_DTYPE),
                           (((1,), (1,)), ((), ())), preferred_element_type=F32)


def _rope_table(pos, heads, half):
    rot = 2 * half
    inv = ROPE_THETA ** (-jnp.arange(half, dtype=F32) * 2.0 / rot)
    freq = np.zeros(LANES, np.int32)
    rotated = np.zeros(LANES, bool)
    upper = np.zeros(LANES, bool)
    for lane0, dh in heads:
        assert dh // 8 == half
        freq[lane0:lane0 + rot] = np.arange(rot) % half
        rotated[lane0:lane0 + rot] = True
        upper[lane0 + half:lane0 + rot] = True
    inv_lane = jnp.where(rotated, inv[freq], 0.0)
    ang = pos.astype(F32)[:, None] * inv_lane[None, :]
    cos, sin = jnp.cos(ang), jnp.sin(ang)
    s1 = jnp.where(upper, sin, 0.0)
    s2 = jnp.where(rotated & ~upper, -sin, 0.0)
    return jnp.stack([cos, s1, s2])


def _rope_chunk(xc, tab_ref, half):
    return (xc * tab_ref[0] + pltpu.roll(xc, half, 1) * tab_ref[1]
            + pltpu.roll(xc, LANES - half, 1) * tab_ref[2])


def _proj_kernel(*refs, n_tabs, halves, groups, scales, tn, with_xn):
    x_ref, g_ref, w_ref = refs[:3]
    tab_refs = refs[3:3 + n_tabs]
    o_ref = refs[3 + n_tabs]
    xn_out = refs[4 + n_tabs] if with_xn else None
    xn_ref = refs[-1]
    j = pl.program_id(1)

    @pl.when(j == 0)
    def _():
        x = x_ref[...]
        ms = jnp.mean(x * x, axis=-1, keepdims=True)
        xn = (x * lax.rsqrt(ms + EPS) * g_ref[...]).astype(MXU_DTYPE)
        xn_ref[...] = xn
        if with_xn:
            xn_out[...] = xn

    acc = jnp.dot(xn_ref[...], w_ref[...], preferred_element_type=F32)

    for (j_lo, j_hi), cfg in groups:
        @pl.when((j >= j_lo) & (j < j_hi))
        def _(cfg=cfg):
            for c in range(tn // LANES):
                xc = acc[:, c * LANES:(c + 1) * LANES]
                if cfg[c] >= 0:
                    xc = _rope_chunk(xc, tab_refs[cfg[c]], halves[cfg[c]])
                if scales[c] != 1.0:
                    xc = xc * scales[c]
                o_ref[:, c * LANES:(c + 1) * LANES] = xc.astype(o_ref.dtype)


def _proj(x, g, w, *, k_cols, tm, tn, tabs=(), halves=(), groups=None, scales=None, with_xn=False,
          out_dtype=F32):
    t = x.shape[0]
    n = w.shape[1]
    nj = n // tn
    if groups is None:
        groups = (((0, nj), (-1,) * (tn // LANES)),)
    if scales is None:
        scales = (1.0,) * (tn // LANES)
    kern = functools.partial(_proj_kernel, n_tabs=len(tabs), halves=tuple(halves),
                             groups=tuple(groups), scales=tuple(scales), tn=tn, with_xn=with_xn)
    in_specs = [pl.BlockSpec((tm, k_cols), lambda i, j: (i, 0)),
                pl.BlockSpec((1, k_cols), lambda i, j: (0, 0)),
                pl.BlockSpec((None, k_cols, tn), lambda i, j: (j, 0, 0),
                             pipeline_mode=_resident(nj))]
    in_specs += [pl.BlockSpec((3, tm, LANES), lambda i, j: (0, i, 0)) for _ in tabs]
    out_shape = [jax.ShapeDtypeStruct((t, n), out_dtype)]
    out_specs = [pl.BlockSpec((tm, tn), lambda i, j: (i, j))]
    if with_xn:
        out_shape.append(jax.ShapeDtypeStruct((t, k_cols), MXU_DTYPE))
        out_specs.append(pl.BlockSpec((tm, k_cols), lambda i, j: (i, 0)))
    res = pl.pallas_call(
        kern, grid=(t // tm, nj), in_specs=in_specs, out_specs=out_specs, out_shape=out_shape,
        scratch_shapes=[pltpu.VMEM((tm, k_cols), MXU_DTYPE)],
        compiler_params=_cparams("parallel", "arbitrary"), name="norm_proj",
    )(x, g.reshape(1, k_cols), _col_tiles(w, tn), *tabs)
    return res if with_xn else res[0]


def _mm_res_kernel(*refs, n_lhs):
    a_refs = refs[:n_lhs]
    w_refs = refs[n_lhs:2 * n_lhs]
    x_ref = refs[2 * n_lhs]
    o_ref = refs[2 * n_lhs + 1]
    ab_refs = refs[2 * n_lhs + 2:]

    @pl.when(pl.program_id(1) == 0)
    def _():
        for a_ref, ab_ref in zip(a_refs, ab_refs):
            ab_ref[...] = a_ref[...].astype(MXU_DTYPE)

    acc = x_ref[...]
    for ab_ref, w_ref in zip(ab_refs, w_refs):
        acc = acc + jnp.dot(ab_ref[...], w_ref[...], preferred_element_type=F32)
    o_ref[...] = acc


def _mm_res(lhs, ws, x, *, tm, tn):
    t, d = x.shape
    n_lhs = len(lhs)
    in_specs = [pl.BlockSpec((tm, a.shape[1]), lambda i, j: (i, 0)) for a in lhs]
    in_specs += [pl.BlockSpec((None, w.shape[0], tn), lambda i, j: (j, 0, 0),
                              pipeline_mode=_resident(d // tn)) for w in ws]
    in_specs += [pl.BlockSpec((tm, tn), lambda i, j: (i, j))]
    return pl.pallas_call(
        functools.partial(_mm_res_kernel, n_lhs=n_lhs), grid=(t // tm, d // tn),
        in_specs=in_specs, out_specs=pl.BlockSpec((tm, tn), lambda i, j: (i, j)),
        out_shape=jax.ShapeDtypeStruct((t, d), F32),
        scratch_shapes=[pltpu.VMEM((tm, a.shape[1]), MXU_DTYPE) for a in lhs],
        compiler_params=_cparams("parallel", "arbitrary"), name="out_proj",
    )(*lhs, *[_col_tiles(w, tn) for w in ws], x)


def _swa_kernel(sink_ref, q_ref, kp_ref, kc_ref, vp_ref, vc_ref, o_ref):
    n = pl.program_id(1)
    g_sz = A_Q_HEADS // A_KV_HEADS
    dh = A_HEAD_DIM
    q = q_ref[...]
    k2 = jnp.concatenate([kp_ref[...], kc_ref[...]], axis=0).astype(MXU_DTYPE)
    v2 = jnp.concatenate([vp_ref[...], vc_ref[...]], axis=0).astype(MXU_DTYPE)
    qi = lax.broadcasted_iota(jnp.int32, (BLOCK, 2 * BLOCK), 0)
    kj = lax.broadcasted_iota(jnp.int32, (BLOCK, 2 * BLOCK), 1)
    diff = qi + BLOCK - kj
    valid = (diff >= 0) & (diff < SWA_WINDOW) & ((kj >= BLOCK) | (n > 0))
    scale = dh ** -0.5
    for h in range(A_KV_HEADS):
        kh = k2[:, h * dh:(h + 1) * dh]
        vh = v2[:, h * dh:(h + 1) * dh]
        for g in range(g_sz):
            hq = h * g_sz + g
            s = _dot_nt(q[:, hq * dh:(hq + 1) * dh], kh) * scale
            s = jnp.where(valid, s, NEG_INF)
            sink = sink_ref[hq]
            m = jnp.maximum(jnp.max(s, axis=-1, keepdims=True), sink)
            p = jnp.exp(s - m)
            denom = jnp.sum(p, axis=-1, keepdims=True) + jnp.exp(sink - m)
            o_ref[:, hq * dh:(hq + 1) * dh] = (_dot(p, vh) / denom).astype(o_ref.dtype)


def _swa(p0, sinks, b, s):
    nb = s // BLOCK
    kv_w = A_KV_COLS
    k_blk = A_Q_COLS // kv_w
    v_blk = k_blk + 1
    cur = lambda bi, n: bi * nb + n
    prev = lambda bi, n: bi * nb + jnp.maximum(n - 1, 0)
    return pl.pallas_call(
        _swa_kernel, grid=(b, nb),
        in_specs=[pl.BlockSpec(memory_space=pltpu.SMEM),
                  pl.BlockSpec((BLOCK, A_Q_COLS), lambda bi, n: (cur(bi, n), 0)),
                  pl.BlockSpec((BLOCK, kv_w), lambda bi, n: (prev(bi, n), k_blk)),
                  pl.BlockSpec((BLOCK, kv_w), lambda bi, n: (cur(bi, n), k_blk)),
                  pl.BlockSpec((BLOCK, kv_w), lambda bi, n: (prev(bi, n), v_blk)),
                  pl.BlockSpec((BLOCK, kv_w), lambda bi, n: (cur(bi, n), v_blk))],
        out_specs=pl.BlockSpec((BLOCK, A_Q_COLS), lambda bi, n: (cur(bi, n), 0)),
        out_shape=jax.ShapeDtypeStruct((b * s, A_Q_COLS), MXU_DTYPE),
        compiler_params=_cparams("parallel", "arbitrary"), name="swa",
    )(sinks, p0, p0, p0, p0, p0)


CONV_ROWS = 256
CONV_COLS = 512
HALO = 8


def _conv_kernel(bg_ref, cg_ref, hx_ref, cgh_ref, hxh_ref, w_ref, o_ref, *, tiles_per_seq):
    i = pl.program_id(0)
    z = cg_ref[...] * hx_ref[...]
    first = (i % tiles_per_seq) == 0
    zh = jnp.where(first, 0.0, cgh_ref[...] * hxh_ref[...])
    row = lax.broadcasted_iota(jnp.int32, z.shape, 0)
    z1 = pltpu.roll(z, 1, 0)
    z1 = jnp.where(row == 0, zh[HALO - 1:HALO], z1)
    z2 = pltpu.roll(z, 2, 0)
    z2 = jnp.where(row == 0, zh[HALO - 2:HALO - 1], jnp.where(row == 1, zh[HALO - 1:HALO], z2))
    w = w_ref[...]
    o_ref[...] = (bg_ref[...] * (w[0:1] * z2 + w[1:2] * z1 + w[2:3] * z)).astype(o_ref.dtype)


def _conv(p0, conv_w, s):
    t = p0.shape[0]
    col0 = A_Q_COLS + 2 * A_KV_COLS
    bg_blk, cg_blk, hx_blk = (col0 // CONV_COLS, (col0 + B_WIDTH) // CONV_COLS,
                              (col0 + 2 * B_WIDTH) // CONV_COLS)
    rh = CONV_ROWS // HALO
    halo_row = lambda i: jnp.maximum(i * rh - 1, 0)
    return pl.pallas_call(
        functools.partial(_conv_kernel, tiles_per_seq=s // CONV_ROWS),
        grid=(t // CONV_ROWS, B_WIDTH // CONV_COLS),
        in_specs=[pl.BlockSpec((CONV_ROWS, CONV_COLS), lambda i, j: (i, bg_blk + j)),
                  pl.BlockSpec((CONV_ROWS, CONV_COLS), lambda i, j: (i, cg_blk + j)),
                  pl.BlockSpec((CONV_ROWS, CONV_COLS), lambda i, j: (i, hx_blk + j)),
                  pl.BlockSpec((HALO, CONV_COLS), lambda i, j: (halo_row(i), cg_blk + j)),
                  pl.BlockSpec((HALO, CONV_COLS), lambda i, j: (halo_row(i), hx_blk + j)),
                  pl.BlockSpec((3, CONV_COLS), lambda i, j: (0, j))],
        out_specs=pl.BlockSpec((CONV_ROWS, CONV_COLS), lambda i, j: (i, j)),
        out_shape=jax.ShapeDtypeStruct((t, B_WIDTH), MXU_DTYPE),
        compiler_params=_cparams("parallel", "parallel"), name="gated_conv",
    )(p0, p0, p0, p0, p0, conv_w)


DSA_WIDTHS = 4
COUNT_CHAINS = 8


def _dsa_block(qa_ref, qb_ref, iq_ref, iwq_ref, k_ref, v_ref, ikw_ref, o_ref, key_ref, half_ref,
               n, w, k_sel):
    ik = ikw_ref[0:w, 0:IDX_HEAD_DIM].astype(MXU_DTYPE)
    iw_t = (iwq_ref[...] * ((IDX_HEADS ** -0.5) * (IDX_HEAD_DIM ** -0.5))).T
    iq = iq_ref[...]
    heads_per_chunk = LANES // IDX_HEAD_DIM
    score = jnp.zeros((w, BLOCK), F32)
    for c in range(IDX_HEADS // heads_per_chunk):
        chunk = iq[:, c * LANES:(c + 1) * LANES]
        rhs = jnp.concatenate([chunk[:, hh * IDX_HEAD_DIM:(hh + 1) * IDX_HEAD_DIM]
                               for hh in range(heads_per_chunk)], axis=0)
        lg = _dot_nt(ik, rhs)
        for hh in range(heads_per_chunk):
            row = IDX_HEAD_DIM + c * heads_per_chunk + hh
            score = score + iw_t[row:row + 1, :] * jnp.maximum(lg[:, hh * BLOCK:(hh + 1) * BLOCK], 0.0)
    score = score + 0.0
    kpos = lax.broadcasted_iota(jnp.int32, (w, BLOCK), 0)
    qpos = n * BLOCK + lax.broadcasted_iota(jnp.int32, (w, BLOCK), 1)
    bits = pltpu.bitcast(score, jnp.int32)
    key = jnp.where(bits < 0, bits ^ jnp.int32(0x7FFFFFFF), bits)
    key = jnp.where(kpos <= qpos, key, INT_MIN)
    key_ref[0:w, :] = key
    half_ref[0:w, :] = jnp.right_shift(key, 16).astype(jnp.int16)

    def search16():
        def count_ge(t):
            t16 = t.astype(jnp.int16)
            accs = [None] * COUNT_CHAINS
            for c in range(w // PACK):
                one = jnp.where(half_ref[c * PACK:(c + 1) * PACK, :] >= t16,
                                jnp.int16(1), jnp.int16(0))
                a = c % COUNT_CHAINS
                accs[a] = one if accs[a] is None else accs[a] + one
            acc = functools.reduce(lambda x, y: x + y, [a for a in accs if a is not None])
            return jnp.sum(acc.astype(F32), axis=0, keepdims=True)

        kf = float(k_sel)
        t0 = jnp.where(count_ge(jnp.zeros((1, BLOCK), jnp.int32)) >= kf,
                       jnp.int32(0), jnp.int32(-2 ** 15))

        def bit_step(it, t):
            cand = t + jnp.left_shift(jnp.int32(1), jnp.int32(14) - it)
            return jnp.where(count_ge(cand) >= kf, cand, t)

        return lax.fori_loop(0, 15, bit_step, t0)

    t_hi = search16()
    hi = half_ref[0:w, :].astype(jnp.int32)
    lo = jnp.bitwise_and(key_ref[0:w, :], 0xFFFF) - 2 ** 15
    half_ref[0:w, :] = jnp.where(hi > t_hi, 2 ** 15 - 1,
                                 jnp.where(hi < t_hi, -2 ** 15, lo)).astype(jnp.int16)
    t_lo = search16()
    thr = jnp.maximum(t_hi * 2 ** 16 + (t_lo + 2 ** 15), INT_MIN + 1)
    bias = jnp.where(key_ref[0:w, :] >= thr, 0.0, NEG_INF).T

    kk = k_ref[0:w, :].astype(MXU_DTYPE)
    lane = lax.broadcasted_iota(jnp.int32, (w, LANES), 1)
    v_aug = jnp.concatenate([v_ref[0:w, :].astype(MXU_DTYPE),
                             jnp.where(lane == 0, 1.0, 0.0).astype(MXU_DTYPE)], axis=1)
    for h in range(C_HEADS):
        q_r = qa_ref if h < C_HEADS // 2 else qb_ref
        hh = h % (C_HEADS // 2)
        s = _dot_nt(q_r[:, hh * C_HEAD_DIM:(hh + 1) * C_HEAD_DIM], kk) + bias
        p = jnp.exp(s - jnp.max(s, axis=-1, keepdims=True))
        pv = _dot(p, v_aug)
        o_ref[:, h * C_HEAD_DIM:(h + 1) * C_HEAD_DIM] = (
            pv[:, :C_HEAD_DIM] / pv[:, C_HEAD_DIM:C_HEAD_DIM + 1]).astype(o_ref.dtype)


def _dsa_kernel(qa_ref, qb_ref, iq_ref, iwq_ref, k_ref, v_ref, ikw_ref, o_ref, key_ref, half_ref,
                *, k_sel):
    n = pl.program_id(1)
    chunk = k_ref.shape[0] // DSA_WIDTHS
    n_chunks = (n * BLOCK + BLOCK - 1) // chunk + 1
    for wi in range(1, DSA_WIDTHS + 1):
        if (wi * chunk) % BLOCK:
            continue

        @pl.when(n_chunks == wi)
        def _(wi=wi):
            _dsa_block(qa_ref, qb_ref, iq_ref, iwq_ref, k_ref, v_ref, ikw_ref, o_ref, key_ref,
                       half_ref, n, wi * chunk, k_sel)


def _dsa(u, p1, b, s):
    nb = s // BLOCK
    k_sel = min(INDEX_TOPK, s // 4)
    half_q = C_HEADS * C_HEAD_DIM // 2
    k_blk = C_Q_RANK // LANES
    row = lambda bi, n: bi * nb + n
    return pl.pallas_call(
        functools.partial(_dsa_kernel, k_sel=k_sel), grid=(b, nb),
        in_specs=[pl.BlockSpec((BLOCK, half_q), lambda bi, n: (row(bi, n), 0)),
                  pl.BlockSpec((BLOCK, half_q), lambda bi, n: (row(bi, n), 1)),
                  pl.BlockSpec((BLOCK, half_q), lambda bi, n: (row(bi, n), 2)),
                  pl.BlockSpec((BLOCK, LANES), lambda bi, n: (row(bi, n), k_blk + 2)),
                  pl.BlockSpec((s, LANES), lambda bi, n: (bi, k_blk)),
                  pl.BlockSpec((s, LANES), lambda bi, n: (bi, k_blk + 1)),
                  pl.BlockSpec((s, LANES), lambda bi, n: (bi, k_blk + 2))],
        out_specs=pl.BlockSpec((BLOCK, C_HEADS * C_HEAD_DIM), lambda bi, n: (row(bi, n), 0)),
        out_shape=jax.ShapeDtypeStruct((b * s, C_HEADS * C_HEAD_DIM), MXU_DTYPE),
        scratch_shapes=[pltpu.VMEM((s, BLOCK), jnp.int32), pltpu.VMEM((s, BLOCK), jnp.int16)],
        compiler_params=_cparams("parallel", "arbitrary"), name="dsa",
    )(u, u, u, p1, p1, p1, p1)


ROUTER_TOKENS = 256
N_FULL_K0 = 8
SUB = 8


def _top_rows(work, k, exact):
    rows, cols = work.shape
    row = lax.broadcasted_iota(jnp.int32, (rows, cols), 0).astype(F32)
    vrow = lax.broadcasted_iota(jnp.int32, (k, cols), 0)
    rank = jnp.full((rows, cols), float(k), F32)
    vals = jnp.zeros((k, cols), F32)
    for r in range(k):
        m = jnp.max(work, axis=0, keepdims=True)
        hit = work == m
        if exact:
            first = jnp.min(jnp.where(hit, row, float(rows)), axis=0, keepdims=True)
            hit = row == first
        rank = jnp.where(hit, float(r), rank)
        work = jnp.where(hit, -jnp.inf, work)
        vals = jnp.where(vrow == r, m, vals)
    return rank, vals


def _count_ranked(rank, k):
    return jnp.sum(jnp.where(rank < float(k), 1.0, 0.0), axis=0, keepdims=True)


def _pair_counts_exact(v0, v1, k):
    cand = jnp.concatenate([v0[k0:k0 + 1] + v1 for k0 in range(N_FULL_K0)]
                           + [v0[N_FULL_K0:] + v1[0:1]], axis=0)
    rank_c, _ = _top_rows(cand, k, True)
    sel = rank_c < float(k)
    cnt = jnp.where(sel, 1.0, 0.0)
    n_rank = jnp.concatenate(
        [jnp.sum(cnt[k0 * k:(k0 + 1) * k], axis=0, keepdims=True) for k0 in range(N_FULL_K0)]
        + [cnt[N_FULL_K0 * k:]], axis=0)
    z = jnp.sum(jnp.where(sel, jnp.exp(cand - cand[0:1]), 0.0), axis=0, keepdims=True)
    return n_rank, z, jnp.sum(cnt, axis=0, keepdims=True)


def _pair_counts_fast(v0, v1, k):
    cols = v0.shape[1]
    row = lax.broadcasted_iota(jnp.int32, (SUB, cols), 0)
    tiles = []
    for k1 in range(SUB):
        n_valid = min(k // (k1 + 1), SUB)
        tiles.append(jnp.where(row < n_valid, v0[0:SUB] + v1[k1:k1 + 1], -jnp.inf))
    tiles.append(v0[0:1] + v1[SUB:])
    tiles.append(v0[SUB:] + v1[0:1])
    cand = jnp.concatenate(tiles, axis=0)
    rank_c, _ = _top_rows(cand, k, False)
    sel = rank_c < float(k)
    cnt = jnp.where(sel, 1.0, 0.0)
    low = cnt[0:SUB]
    for k1 in range(1, SUB):
        low = low + cnt[k1 * SUB:(k1 + 1) * SUB]
    tail = jnp.sum(cnt[SUB * SUB:SUB * SUB + SUB], axis=0, keepdims=True)
    low = low + jnp.where(row == 0, tail, 0.0)
    n_rank = jnp.concatenate([low, cnt[SUB * SUB + SUB:]], axis=0)
    z = jnp.sum(jnp.where(sel, jnp.exp(cand - (v0[0:1] + v1[0:1])), 0.0), axis=0, keepdims=True)
    return n_rank, z, jnp.sum(cnt, axis=0, keepdims=True)


def _router_kernel(q_ref, keys_ref, p0_ref, n0_ref, p1_ref, r1_ref):
    k = PEER_TOPK

    def head(h, carry):
        col0 = pl.multiple_of(h * (2 * PEER_HALF), LANES)
        col1 = pl.multiple_of(col0 + PEER_HALF, LANES)
        s0 = _dot_nt(keys_ref[2 * h], q_ref[:, pl.ds(col0, PEER_HALF)])
        s1 = _dot_nt(keys_ref[2 * h + 1], q_ref[:, pl.ds(col1, PEER_HALF)])

        def route(exact):
            rank0, v0 = _top_rows(s0, k, exact)
            rank1, v1 = _top_rows(s1, k, exact)
            n_rank, z, n_sel = (_pair_counts_exact if exact else _pair_counts_fast)(v0, v1, k)
            n0 = jnp.zeros_like(s0)
            for r in range(k):
                n0 = jnp.where(rank0 == float(r), n_rank[r:r + 1], n0)
            p0_ref[h] = jnp.where(rank0 < float(k), jnp.exp(s0 - v0[0:1]), 0.0) / z
            n0_ref[h] = n0
            p1_ref[h] = jnp.where(rank1 < float(k), jnp.exp(s1 - v1[0:1]), 0.0).astype(p1_ref.dtype)
            r1_ref[h] = rank1.astype(r1_ref.dtype)
            return _count_ranked(rank0, k) + _count_ranked(rank1, k) + n_sel

        n_marked = route(False)

        @pl.when(jnp.max(n_marked) > 3.0 * k)
        def _():
            route(True)

        return carry

    lax.fori_loop(0, PEER_HEADS, head, 0)


def _router(q, keys):
    t = q.shape[0]
    tb = ROUTER_TOKENS
    blk = pl.BlockSpec((PEER_HEADS, PEER_N_KEYS, tb), lambda i: (0, 0, i))
    f32s = jax.ShapeDtypeStruct((PEER_HEADS, PEER_N_KEYS, t), F32)
    bf16s = jax.ShapeDtypeStruct((PEER_HEADS, PEER_N_KEYS, t), BF16)
    return pl.pallas_call(
        _router_kernel, grid=(t // tb,),
        in_specs=[pl.BlockSpec((tb, q.shape[1]), lambda i: (i, 0)),
                  pl.BlockSpec(keys.shape, lambda i: (0, 0, 0))],
        out_specs=[blk, blk, blk, blk], out_shape=[f32s, f32s, bf16s, bf16s],
        compiler_params=_cparams("parallel"), name="peer_router",
    )(q, keys)


EXPERT_TOKENS = 512
EXPERT_CHUNK = 1024
ROWS_PER_CHUNK = EXPERT_CHUNK // PEER_N_KEYS


MM1_SLICES = 2
PACK = 16


def _gate_tile(pre_ref, p0_ref, n0_ref, p1_ref, r1_ref, h_ref, tile, cache):
    tb = pre_ref.shape[1]
    ii, rt = divmod(tile, PEER_N_KEYS // PACK)
    if ii not in cache:
        cache.clear()
        cache[ii] = (
            [jnp.broadcast_to(p0_ref[h, ii:ii + 1, :], (PACK, tb)).astype(BF16)
             for h in range(PEER_HEADS)],
            [jnp.broadcast_to(n0_ref[h, ii:ii + 1, :], (PACK, tb)).astype(BF16)
             for h in range(PEER_HEADS)])
    gates, cnts = cache[ii]
    js = slice(rt * PACK, (rt + 1) * PACK)
    w = None
    for h in range(PEER_HEADS):
        term = jnp.where(r1_ref[h, js, :] < cnts[h], p1_ref[h, js, :],
                         jnp.zeros((), BF16)) * gates[h]
        w = term if w is None else w + term
    rows = slice(tile * PACK, (tile + 1) * PACK)
    pre = pre_ref[rows, :]
    act = 0.5 * pre * (1.0 + lax.erf(pre * np.sqrt(0.5).astype(np.float32)))
    h_ref[rows, :] = act.astype(BF16) * w


def _expert_kernel(xn_ref, x_ref, u_ref, vt_ref, p0_ref, n0_ref, p1_ref, r1_ref, o_ref, acc_ref,
                   pre_ref, h_ref):
    c = pl.program_id(1)

    @pl.when(c == 0)
    def _():
        acc_ref[...] = jnp.zeros_like(acc_ref)

    cache = {}
    n_tiles = EXPERT_CHUNK // PACK
    for k in range(MM1_SLICES):
        rows = slice(k * EXPERT_CHUNK // MM1_SLICES, (k + 1) * EXPERT_CHUNK // MM1_SLICES)
        pre_ref[rows, :] = _dot_nt(u_ref[rows, :], xn_ref[...])
        for tile in range(k * n_tiles // MM1_SLICES, (k + 1) * n_tiles // MM1_SLICES):
            _gate_tile(pre_ref, p0_ref, n0_ref, p1_ref, r1_ref, h_ref, tile, cache)
    acc_ref[...] += jnp.dot(vt_ref[...], h_ref[...], preferred_element_type=F32)

    @pl.when(c == pl.num_programs(1) - 1)
    def _():
        o_ref[...] = x_ref[...] + acc_ref[...].T


def _experts(xn, x, u, vt, p0, n0, p1, r1):
    t, d = x.shape
    tb, ec = EXPERT_TOKENS, EXPERT_CHUNK
    small = pl.BlockSpec((PEER_HEADS, ROWS_PER_CHUNK, tb), lambda i, c: (0, c, i))
    full = pl.BlockSpec((PEER_HEADS, PEER_N_KEYS, tb), lambda i, c: (0, 0, i))
    return pl.pallas_call(
        _expert_kernel, grid=(t // tb, PEER_EXPERTS // ec),
        in_specs=[pl.BlockSpec((tb, d), lambda i, c: (i, 0)),
                  pl.BlockSpec((tb, d), lambda i, c: (i, 0)),
                  pl.BlockSpec((ec, d), lambda i, c: (c, 0)),
                  pl.BlockSpec((None, d, ec), lambda i, c: (c, 0, 0)),
                  small, small, full, full],
        out_specs=pl.BlockSpec((tb, d), lambda i, c: (i, 0)),
        out_shape=jax.ShapeDtypeStruct((t, d), F32),
        scratch_shapes=[pltpu.VMEM((d, tb), F32), pltpu.VMEM((ec, tb), F32),
                        pltpu.VMEM((ec, tb), BF16)],
        compiler_params=_cparams("parallel", "arbitrary"), name="peer_experts",
    )(xn, x, u, vt, p0, n0, p1, r1)


def _peer(x, g, wq, keys, u, v):
    xq, xn = _proj(x, g, wq.astype(MXU_DTYPE), k_cols=x.shape[1], tm=512, tn=wq.shape[1],
                   with_xn=True, out_dtype=MXU_DTYPE)
    keys2 = keys.reshape(2 * PEER_HEADS, PEER_N_KEYS, PEER_HALF).astype(MXU_DTYPE)
    p0, n0, p1, r1 = _router(xq, keys2)
    nc = PEER_EXPERTS // EXPERT_CHUNK
    vt = v.astype(MXU_DTYPE).reshape(nc, EXPERT_CHUNK, -1).transpose(0, 2, 1)
    return _experts(xn, x, u.astype(MXU_DTYPE), vt, p0, n0, p1, r1)


def _norm_kernel(x_ref, g_ref, o_ref):
    x = x_ref[...]
    ms = jnp.mean(x * x, axis=-1, keepdims=True)
    o_ref[...] = x * lax.rsqrt(ms + EPS) * g_ref[...]


def _norm(x, g, tm=512):
    t, d = x.shape
    return pl.pallas_call(
        _norm_kernel, grid=(t // tm,),
        in_specs=[pl.BlockSpec((tm, d), lambda i: (i, 0)), pl.BlockSpec((1, d), lambda i: (0, 0))],
        out_specs=pl.BlockSpec((tm, d), lambda i: (i, 0)),
        out_shape=jax.ShapeDtypeStruct((t, d), F32),
        compiler_params=_cparams("parallel"), name="final_norm",
    )(x, g.reshape(1, d))


def kernel(x, positions, norm_mix0, w_in0, sinks0, conv_w0, w_out0, norm_ffn0, peer_wq0, peer_keys0,
           peer_u0, peer_v0, norm_mix1, w_in1, g_qa1, w_uq1, w_out1, norm_ffn1, peer_wq1,
           peer_keys1, peer_u1, peer_v1, norm_final):
    b, s, d = x.shape
    t = b * s
    xf = x.reshape(t, d)
    pos = positions.reshape(t)
    tab_a = _rope_table(pos, ((0, 64), (64, 64)), 8)
    tab_b = _rope_table(pos, ((0, 128),), 16)
    tab_c = _rope_table(pos, ((0, 64),), 8)

    n0 = w_in0.shape[1]
    n_rope = (A_Q_COLS + A_KV_COLS) // LANES
    cfg0 = (0,) * n_rope + (-1,) * (n0 // LANES - n_rope)
    p0 = _proj(xf, norm_mix0, w_in0.astype(MXU_DTYPE), k_cols=d, tm=256, tn=n0,
               tabs=(tab_a,), halves=(8,), groups=(((0, 1), cfg0),))
    a_out = _swa(p0, sinks0, b, s)
    b_out = _conv(p0, conv_w0, s)
    w_o = w_out0.astype(MXU_DTYPE)
    x1 = _mm_res([a_out, b_out], [w_o[:A_Q_COLS], w_o[A_Q_COLS:]], xf, tm=512, tn=d)
    x2 = _peer(x1, norm_ffn0, peer_wq0, peer_keys0, peer_u0, peer_v0)

    w1 = jnp.pad(w_in1, ((0, 0), (0, IN1_COLS_PADDED - w_in1.shape[1]))).astype(MXU_DTYPE)
    k_chunk = C_Q_RANK // LANES
    cfg1 = tuple({k_chunk: 0, k_chunk + 2: 1}.get(c, -1) for c in range(IN1_COLS_PADDED // LANES))
    p1 = _proj(x2, norm_mix1, w1, k_cols=d, tm=512, tn=IN1_COLS_PADDED,
               tabs=(tab_b, tab_c), halves=(16, 8), groups=(((0, 1), cfg1),))
    n_q = C_HEADS * C_HEAD_DIM // LANES
    n_iq = IDX_HEADS * IDX_HEAD_DIM // LANES
    cfg_u = (0,) * n_q + (1,) * n_iq
    u = _proj(p1, g_qa1, w_uq1.astype(MXU_DTYPE), k_cols=C_Q_RANK, tm=256,
              tn=(n_q + n_iq) * LANES, tabs=(tab_b, tab_a), halves=(16, 8),
              groups=(((0, 1), cfg_u),), scales=(C_HEAD_DIM ** -0.5,) * n_q + (1.0,) * n_iq,
              out_dtype=MXU_DTYPE)
    o = _dsa(u, p1, b, s)
    x3 = _mm_res([o], [w_out1.astype(MXU_DTYPE)], x2, tm=512, tn=d)
    x4 = _peer(x3, norm_ffn1, peer_wq1, peer_keys1, peer_u1, peer_v1)
    return _norm(x4, norm_final).reshape(b, s, d)
```

```python
import functools

import numpy as np
import jax
import jax.numpy as jnp
from jax import lax
from jax.experimental import pallas as pl
from jax.experimental.pallas import tpu as pltpu

F32 = jnp.float32
BF16 = jnp.bfloat16
MXU_DTYPE = jnp.bfloat16

LANES = 128
EPS = 1e-5
ROPE_THETA = 500000.0
BLOCK = 128
NEG_INF = -1e30
INT_MIN = -(2 ** 31)

SWA_WINDOW = 128
A_Q_HEADS, A_KV_HEADS, A_HEAD_DIM = 16, 4, 64
A_Q_COLS = A_Q_HEADS * A_HEAD_DIM
A_KV_COLS = A_KV_HEADS * A_HEAD_DIM
B_WIDTH = 1024
C_Q_RANK, C_HEADS, C_HEAD_DIM = 512, 16, 128
IDX_HEADS, IDX_HEAD_DIM, INDEX_TOPK = 16, 64, 256
IN1_COLS_PADDED = 896
PEER_HEADS, PEER_N_KEYS, PEER_TOPK, PEER_HALF = 8, 128, 16, 128
PEER_EXPERTS = PEER_N_KEYS * PEER_N_KEYS

VMEM_LIMIT = 56 * 1024 * 1024


def _cparams(*sem):
    return pltpu.CompilerParams(dimension_semantics=sem, vmem_limit_bytes=VMEM_LIMIT)


def _resident(n_tiles):
    return pl.Buffered(1) if n_tiles == 1 else None


def _col_tiles(w, tn):
    k, n = w.shape
    return w.reshape(k, n // tn, tn).transpose(1, 0, 2)


def _dot(a, b):
    return jnp.dot(a.astype(MXU_DTYPE), b.astype(MXU_DTYPE), preferred_element_type=F32)


def _dot_nt(a, b):
    return lax.dot_general(a.astype(MXU_DTYPE), b.astype(MXU_DTYPE),
                           (((1,), (1,)), ((), ())), preferred_element_type=F32)


def _rope_table(pos, heads, half):
    rot = 2 * half
    inv = ROPE_THETA ** (-jnp.arange(half, dtype=F32) * 2.0 / rot)
    freq = np.zeros(LANES, np.int32)
    rotated = np.zeros(LANES, bool)
    upper = np.zeros(LANES, bool)
    for lane0, dh in heads:
        assert dh // 8 == half
        freq[lane0:lane0 + rot] = np.arange(rot) % half
        rotated[lane0:lane0 + rot] = True
        upper[lane0 + half:lane0 + rot] = True
    ang = pos.astype(F32)[:, None] * inv[None, :]
    cos, sin = jnp.cos(ang), jnp.sin(ang)
    onehot = (np.arange(half)[:, None] == freq[None, :]) & rotated[None, :]
    spread = lambda v, m: jnp.dot(v, jnp.asarray(onehot & m[None, :], F32),
                                  precision=lax.Precision.HIGHEST)
    c = spread(cos, rotated) + jnp.asarray(~rotated, F32)
    return jnp.stack([c, spread(sin, upper), spread(-sin, rotated & ~upper)])


def _rope_chunk(xc, tab_ref, half):
    return (xc * tab_ref[0] + pltpu.roll(xc, half, 1) * tab_ref[1]
            + pltpu.roll(xc, LANES - half, 1) * tab_ref[2])


def _proj_kernel(*refs, n_tabs, halves, groups, scales, tn, with_xn):
    x_ref, g_ref, w_ref = refs[:3]
    tab_refs = refs[3:3 + n_tabs]
    o_ref = refs[3 + n_tabs]
    xn_out = refs[4 + n_tabs] if with_xn else None
    xn_ref = refs[-1]
    j = pl.program_id(1)

    @pl.when(j == 0)
    def _():
        x = x_ref[...]
        ms = jnp.mean(x * x, axis=-1, keepdims=True)
        xn = (x * lax.rsqrt(ms + EPS) * g_ref[...]).astype(MXU_DTYPE)
        xn_ref[...] = xn
        if with_xn:
            xn_out[...] = xn

    acc = jnp.dot(xn_ref[...], w_ref[...], preferred_element_type=F32)

    for (j_lo, j_hi), cfg in groups:
        @pl.when((j >= j_lo) & (j < j_hi))
        def _(cfg=cfg):
            for c in range(tn // LANES):
                xc = acc[:, c * LANES:(c + 1) * LANES]
                if cfg[c] >= 0:
                    xc = _rope_chunk(xc, tab_refs[cfg[c]], halves[cfg[c]])
                if scales[c] != 1.0:
                    xc = xc * scales[c]
                o_ref[:, c * LANES:(c + 1) * LANES] = xc.astype(o_ref.dtype)


def _proj(x, g, w, *, k_cols, tm, tn, tabs=(), halves=(), groups=None, scales=None, with_xn=False,
          out_dtype=F32):
    t = x.shape[0]
    n = w.shape[1]
    nj = n // tn
    if groups is None:
        groups = (((0, nj), (-1,) * (tn // LANES)),)
    if scales is None:
        scales = (1.0,) * (tn // LANES)
    kern = functools.partial(_proj_kernel, n_tabs=len(tabs), halves=tuple(halves),
                             groups=tuple(groups), scales=tuple(scales), tn=tn, with_xn=with_xn)
    in_specs = [pl.BlockSpec((tm, k_cols), lambda i, j: (i, 0)),
                pl.BlockSpec((1, k_cols), lambda i, j: (0, 0)),
                pl.BlockSpec((None, k_cols, tn), lambda i, j: (j, 0, 0),
                             pipeline_mode=_resident(nj))]
    in_specs += [pl.BlockSpec((3, tm, LANES), lambda i, j: (0, i, 0)) for _ in tabs]
    out_shape = [jax.ShapeDtypeStruct((t, n), out_dtype)]
    out_specs = [pl.BlockSpec((tm, tn), lambda i, j: (i, j))]
    if with_xn:
        out_shape.append(jax.ShapeDtypeStruct((t, k_cols), MXU_DTYPE))
        out_specs.append(pl.BlockSpec((tm, k_cols), lambda i, j: (i, 0)))
    res = pl.pallas_call(
        kern, grid=(t // tm, nj), in_specs=in_specs, out_specs=out_specs, out_shape=out_shape,
        scratch_shapes=[pltpu.VMEM((tm, k_cols), MXU_DTYPE)],
        compiler_params=_cparams("parallel", "arbitrary"), name="norm_proj",
    )(x, g.reshape(1, k_cols), _col_tiles(w, tn), *tabs)
    return res if with_xn else res[0]


def _mm_res_kernel(*refs, n_lhs):
    a_refs = refs[:n_lhs]
    w_refs = refs[n_lhs:2 * n_lhs]
    x_ref = refs[2 * n_lhs]
    o_ref = refs[2 * n_lhs + 1]
    ab_refs = refs[2 * n_lhs + 2:]

    @pl.when(pl.program_id(1) == 0)
    def _():
        for a_ref, ab_ref in zip(a_refs, ab_refs):
            ab_ref[...] = a_ref[...].astype(MXU_DTYPE)

    acc = x_ref[...]
    for ab_ref, w_ref in zip(ab_refs, w_refs):
        acc = acc + jnp.dot(ab_ref[...], w_ref[...], preferred_element_type=F32)
    o_ref[...] = acc


def _mm_res(lhs, ws, x, *, tm, tn):
    t, d = x.shape
    n_lhs = len(lhs)
    in_specs = [pl.BlockSpec((tm, a.shape[1]), lambda i, j: (i, 0)) for a in lhs]
    in_specs += [pl.BlockSpec((None, w.shape[0], tn), lambda i, j: (j, 0, 0),
                              pipeline_mode=_resident(d // tn)) for w in ws]
    in_specs += [pl.BlockSpec((tm, tn), lambda i, j: (i, j))]
    return pl.pallas_call(
        functools.partial(_mm_res_kernel, n_lhs=n_lhs), grid=(t // tm, d // tn),
        in_specs=in_specs, out_specs=pl.BlockSpec((tm, tn), lambda i, j: (i, j)),
        out_shape=jax.ShapeDtypeStruct((t, d), F32),
        scratch_shapes=[pltpu.VMEM((tm, a.shape[1]), MXU_DTYPE) for a in lhs],
        compiler_params=_cparams("parallel", "arbitrary"), name="out_proj",
    )(*lhs, *[_col_tiles(w, tn) for w in ws], x)


def _swa_kernel(sink_ref, q_ref, kp_ref, kc_ref, vp_ref, vc_ref, o_ref):
    n = pl.program_id(1)
    g_sz = A_Q_HEADS // A_KV_HEADS
    dh = A_HEAD_DIM
    q = q_ref[...]
    k2 = jnp.concatenate([kp_ref[...], kc_ref[...]], axis=0).astype(MXU_DTYPE)
    v2 = jnp.concatenate([vp_ref[...], vc_ref[...]], axis=0).astype(MXU_DTYPE)
    qi = lax.broadcasted_iota(jnp.int32, (BLOCK, 2 * BLOCK), 0)
    kj = lax.broadcasted_iota(jnp.int32, (BLOCK, 2 * BLOCK), 1)
    diff = qi + BLOCK - kj
    valid = (diff >= 0) & (diff < SWA_WINDOW) & ((kj >= BLOCK) | (n > 0))
    scale = dh ** -0.5
    for h in range(A_KV_HEADS):
        kh = k2[:, h * dh:(h + 1) * dh]
        vh = v2[:, h * dh:(h + 1) * dh]
        for g in range(g_sz):
            hq = h * g_sz + g
            s = _dot_nt(q[:, hq * dh:(hq + 1) * dh], kh) * scale
            s = jnp.where(valid, s, NEG_INF)
            sink = sink_ref[hq]
            m = jnp.maximum(jnp.max(s, axis=-1, keepdims=True), sink)
            p = jnp.exp(s - m)
            denom = jnp.sum(p, axis=-1, keepdims=True) + jnp.exp(sink - m)
            o_ref[:, hq * dh:(hq + 1) * dh] = (_dot(p, vh) / denom).astype(o_ref.dtype)


def _swa(p0, sinks, b, s):
    nb = s // BLOCK
    kv_w = A_KV_COLS
    k_blk = A_Q_COLS // kv_w
    v_blk = k_blk + 1
    cur = lambda bi, n: bi * nb + n
    prev = lambda bi, n: bi * nb + jnp.maximum(n - 1, 0)
    return pl.pallas_call(
        _swa_kernel, grid=(b, nb),
        in_specs=[pl.BlockSpec(memory_space=pltpu.SMEM),
                  pl.BlockSpec((BLOCK, A_Q_COLS), lambda bi, n: (cur(bi, n), 0)),
                  pl.BlockSpec((BLOCK, kv_w), lambda bi, n: (prev(bi, n), k_blk)),
                  pl.BlockSpec((BLOCK, kv_w), lambda bi, n: (cur(bi, n), k_blk)),
                  pl.BlockSpec((BLOCK, kv_w), lambda bi, n: (prev(bi, n), v_blk)),
                  pl.BlockSpec((BLOCK, kv_w), lambda bi, n: (cur(bi, n), v_blk))],
        out_specs=pl.BlockSpec((BLOCK, A_Q_COLS), lambda bi, n: (cur(bi, n), 0)),
        out_shape=jax.ShapeDtypeStruct((b * s, A_Q_COLS), MXU_DTYPE),
        compiler_params=_cparams("parallel", "arbitrary"), name="swa",
    )(sinks, p0, p0, p0, p0, p0)


CONV_ROWS = 256
CONV_COLS = 512
HALO = 8


def _conv_kernel(bg_ref, cg_ref, hx_ref, cgh_ref, hxh_ref, w_ref, o_ref, *, tiles_per_seq):
    i = pl.program_id(0)
    z = cg_ref[...] * hx_ref[...]
    first = (i % tiles_per_seq) == 0
    zh = jnp.where(first, 0.0, cgh_ref[...] * hxh_ref[...])
    row = lax.broadcasted_iota(jnp.int32, z.shape, 0)
    z1 = pltpu.roll(z, 1, 0)
    z1 = jnp.where(row == 0, zh[HALO - 1:HALO], z1)
    z2 = pltpu.roll(z, 2, 0)
    z2 = jnp.where(row == 0, zh[HALO - 2:HALO - 1], jnp.where(row == 1, zh[HALO - 1:HALO], z2))
    w = w_ref[...]
    o_ref[...] = (bg_ref[...] * (w[0:1] * z2 + w[1:2] * z1 + w[2:3] * z)).astype(o_ref.dtype)


def _conv(p0, conv_w, s):
    t = p0.shape[0]
    col0 = A_Q_COLS + 2 * A_KV_COLS
    bg_blk, cg_blk, hx_blk = (col0 // CONV_COLS, (col0 + B_WIDTH) // CONV_COLS,
                              (col0 + 2 * B_WIDTH) // CONV_COLS)
    rh = CONV_ROWS // HALO
    halo_row = lambda i: jnp.maximum(i * rh - 1, 0)
    return pl.pallas_call(
        functools.partial(_conv_kernel, tiles_per_seq=s // CONV_ROWS),
        grid=(t // CONV_ROWS, B_WIDTH // CONV_COLS),
        in_specs=[pl.BlockSpec((CONV_ROWS, CONV_COLS), lambda i, j: (i, bg_blk + j)),
                  pl.BlockSpec((CONV_ROWS, CONV_COLS), lambda i, j: (i, cg_blk + j)),
                  pl.BlockSpec((CONV_ROWS, CONV_COLS), lambda i, j: (i, hx_blk + j)),
                  pl.BlockSpec((HALO, CONV_COLS), lambda i, j: (halo_row(i), cg_blk + j)),
                  pl.BlockSpec((HALO, CONV_COLS), lambda i, j: (halo_row(i), hx_blk + j)),
                  pl.BlockSpec((3, CONV_COLS), lambda i, j: (0, j))],
        out_specs=pl.BlockSpec((CONV_ROWS, CONV_COLS), lambda i, j: (i, j)),
        out_shape=jax.ShapeDtypeStruct((t, B_WIDTH), MXU_DTYPE),
        compiler_params=_cparams("parallel", "parallel"), name="gated_conv",
    )(p0, p0, p0, p0, p0, conv_w)


DSA_WIDTHS = 4
COUNT_CHAINS = 8


def _dsa_block(qa_ref, qb_ref, iq_ref, iwq_ref, k_ref, v_ref, ikw_ref, o_ref, key_ref, half_ref,
               n, w, k_sel):
    ik = ikw_ref[0:w, 0:IDX_HEAD_DIM].astype(MXU_DTYPE)
    iw_t = (iwq_ref[...] * ((IDX_HEADS ** -0.5) * (IDX_HEAD_DIM ** -0.5))).T
    iq = iq_ref[...]
    heads_per_chunk = LANES // IDX_HEAD_DIM
    score = jnp.zeros((w, BLOCK), F32)
    for c in range(IDX_HEADS // heads_per_chunk):
        chunk = iq[:, c * LANES:(c + 1) * LANES]
        rhs = jnp.concatenate([chunk[:, hh * IDX_HEAD_DIM:(hh + 1) * IDX_HEAD_DIM]
                               for hh in range(heads_per_chunk)], axis=0)
        lg = _dot_nt(ik, rhs)
        for hh in range(heads_per_chunk):
            row = IDX_HEAD_DIM + c * heads_per_chunk + hh
            score = score + iw_t[row:row + 1, :] * jnp.maximum(lg[:, hh * BLOCK:(hh + 1) * BLOCK], 0.0)
    score = score + 0.0
    kpos = lax.broadcasted_iota(jnp.int32, (w, BLOCK), 0)
    qpos = n * BLOCK + lax.broadcasted_iota(jnp.int32, (w, BLOCK), 1)
    bits = pltpu.bitcast(score, jnp.int32)
    key = jnp.where(bits < 0, bits ^ jnp.int32(0x7FFFFFFF), bits)
    key = jnp.where(kpos <= qpos, key, INT_MIN)
    key_ref[0:w, :] = key
    half_ref[0:w, :] = jnp.right_shift(key, 16).astype(jnp.int16)

    def search16():
        def count_ge(t):
            t16 = t.astype(jnp.int16)
            accs = [None] * COUNT_CHAINS
            for c in range(w // PACK):
                one = jnp.where(half_ref[c * PACK:(c + 1) * PACK, :] >= t16,
                                jnp.int16(1), jnp.int16(0))
                a = c % COUNT_CHAINS
                accs[a] = one if accs[a] is None else accs[a] + one
            acc = functools.reduce(lambda x, y: x + y, [a for a in accs if a is not None])
            return jnp.sum(acc.astype(F32), axis=0, keepdims=True)

        kf = float(k_sel)
        t0 = jnp.where(count_ge(jnp.zeros((1, BLOCK), jnp.int32)) >= kf,
                       jnp.int32(0), jnp.int32(-2 ** 15))

        def bit_step(it, t):
            cand = t + jnp.left_shift(jnp.int32(1), jnp.int32(14) - it)
            return jnp.where(count_ge(cand) >= kf, cand, t)

        return lax.fori_loop(0, 15, bit_step, t0)

    t_hi = search16()
    hi = half_ref[0:w, :].astype(jnp.int32)
    lo = jnp.bitwise_and(key_ref[0:w, :], 0xFFFF) - 2 ** 15
    half_ref[0:w, :] = jnp.where(hi > t_hi, 2 ** 15 - 1,
                                 jnp.where(hi < t_hi, -2 ** 15, lo)).astype(jnp.int16)
    t_lo = search16()
    thr = jnp.maximum(t_hi * 2 ** 16 + (t_lo + 2 ** 15), INT_MIN + 1)
    bias = jnp.where(key_ref[0:w, :] >= thr, 0.0, NEG_INF).T

    kk = k_ref[0:w, :].astype(MXU_DTYPE)
    lane = lax.broadcasted_iota(jnp.int32, (w, LANES), 1)
    v_aug = jnp.concatenate([v_ref[0:w, :].astype(MXU_DTYPE),
                             jnp.where(lane == 0, 1.0, 0.0).astype(MXU_DTYPE)], axis=1)
    for h in range(C_HEADS):
        q_r = qa_ref if h < C_HEADS // 2 else qb_ref
        hh = h % (C_HEADS // 2)
        s = _dot_nt(q_r[:, hh * C_HEAD_DIM:(hh + 1) * C_HEAD_DIM], kk) + bias
        p = jnp.exp(s - jnp.max(s, axis=-1, keepdims=True))
        pv = _dot(p, v_aug)
        o_ref[:, h * C_HEAD_DIM:(h + 1) * C_HEAD_DIM] = (
            pv[:, :C_HEAD_DIM] / pv[:, C_HEAD_DIM:C_HEAD_DIM + 1]).astype(o_ref.dtype)


def _dsa_kernel(qa_ref, qb_ref, iq_ref, iwq_ref, k_ref, v_ref, ikw_ref, o_ref, key_ref, half_ref,
                *, k_sel):
    n = pl.program_id(1)
    chunk = k_ref.shape[0] // DSA_WIDTHS
    n_chunks = (n * BLOCK + BLOCK - 1) // chunk + 1
    for wi in range(1, DSA_WIDTHS + 1):
        if (wi * chunk) % BLOCK:
            continue

        @pl.when(n_chunks == wi)
        def _(wi=wi):
            _dsa_block(qa_ref, qb_ref, iq_ref, iwq_ref, k_ref, v_ref, ikw_ref, o_ref, key_ref,
                       half_ref, n, wi * chunk, k_sel)


def _dsa(u, p1, b, s):
    nb = s // BLOCK
    k_sel = min(INDEX_TOPK, s // 4)
    half_q = C_HEADS * C_HEAD_DIM // 2
    k_blk = C_Q_RANK // LANES
    row = lambda bi, n: bi * nb + n
    return pl.pallas_call(
        functools.partial(_dsa_kernel, k_sel=k_sel), grid=(b, nb),
        in_specs=[pl.BlockSpec((BLOCK, half_q), lambda bi, n: (row(bi, n), 0)),
                  pl.BlockSpec((BLOCK, half_q), lambda bi, n: (row(bi, n), 1)),
                  pl.BlockSpec((BLOCK, half_q), lambda bi, n: (row(bi, n), 2)),
                  pl.BlockSpec((BLOCK, LANES), lambda bi, n: (row(bi, n), k_blk + 2)),
                  pl.BlockSpec((s, LANES), lambda bi, n: (bi, k_blk)),
                  pl.BlockSpec((s, LANES), lambda bi, n: (bi, k_blk + 1)),
                  pl.BlockSpec((s, LANES), lambda bi, n: (bi, k_blk + 2))],
        out_specs=pl.BlockSpec((BLOCK, C_HEADS * C_HEAD_DIM), lambda bi, n: (row(bi, n), 0)),
        out_shape=jax.ShapeDtypeStruct((b * s, C_HEADS * C_HEAD_DIM), MXU_DTYPE),
        scratch_shapes=[pltpu.VMEM((s, BLOCK), jnp.int32), pltpu.VMEM((s, BLOCK), jnp.int16)],
        compiler_params=_cparams("parallel", "arbitrary"), name="dsa",
    )(u, u, u, p1, p1, p1, p1)


ROUTER_TOKENS = 256
N_FULL_K0 = 8
SUB = 8


def _top_rows(work, k, exact):
    rows, cols = work.shape
    row = lax.broadcasted_iota(jnp.int32, (rows, cols), 0).astype(F32)
    vrow = lax.broadcasted_iota(jnp.int32, (k, cols), 0)
    rank = jnp.full((rows, cols), float(k), F32)
    vals = jnp.zeros((k, cols), F32)
    for r in range(k):
        m = jnp.max(work, axis=0, keepdims=True)
        hit = work == m
        if exact:
            first = jnp.min(jnp.where(hit, row, float(rows)), axis=0, keepdims=True)
            hit = row == first
        rank = jnp.where(hit, float(r), rank)
        work = jnp.where(hit, -jnp.inf, work)
        vals = jnp.where(vrow == r, m, vals)
    return rank, vals


def _count_ranked(rank, k):
    return jnp.sum(jnp.where(rank < float(k), 1.0, 0.0), axis=0, keepdims=True)


def _pair_counts_exact(v0, v1, k):
    cand = jnp.concatenate([v0[k0:k0 + 1] + v1 for k0 in range(N_FULL_K0)]
                           + [v0[N_FULL_K0:] + v1[0:1]], axis=0)
    rank_c, _ = _top_rows(cand, k, True)
    sel = rank_c < float(k)
    cnt = jnp.where(sel, 1.0, 0.0)
    n_rank = jnp.concatenate(
        [jnp.sum(cnt[k0 * k:(k0 + 1) * k], axis=0, keepdims=True) for k0 in range(N_FULL_K0)]
        + [cnt[N_FULL_K0 * k:]], axis=0)
    z = jnp.sum(jnp.where(sel, jnp.exp(cand - cand[0:1]), 0.0), axis=0, keepdims=True)
    return n_rank, z, jnp.sum(cnt, axis=0, keepdims=True)


def _pair_counts_fast(v0, v1, k):
    cols = v0.shape[1]
    row = lax.broadcasted_iota(jnp.int32, (SUB, cols), 0)
    tiles = []
    for k1 in range(SUB):
        n_valid = min(k // (k1 + 1), SUB)
        tiles.append(jnp.where(row < n_valid, v0[0:SUB] + v1[k1:k1 + 1], -jnp.inf))
    tiles.append(v0[0:1] + v1[SUB:])
    tiles.append(v0[SUB:] + v1[0:1])
    cand = jnp.concatenate(tiles, axis=0)
    rank_c, _ = _top_rows(cand, k, False)
    sel = rank_c < float(k)
    cnt = jnp.where(sel, 1.0, 0.0)
    low = cnt[0:SUB]
    for k1 in range(1, SUB):
        low = low + cnt[k1 * SUB:(k1 + 1) * SUB]
    tail = jnp.sum(cnt[SUB * SUB:SUB * SUB + SUB], axis=0, keepdims=True)
    low = low + jnp.where(row == 0, tail, 0.0)
    n_rank = jnp.concatenate([low, cnt[SUB * SUB + SUB:]], axis=0)
    z = jnp.sum(jnp.where(sel, jnp.exp(cand - (v0[0:1] + v1[0:1])), 0.0), axis=0, keepdims=True)
    return n_rank, z, jnp.sum(cnt, axis=0, keepdims=True)


def _router_kernel(q_ref, keys_ref, p0_ref, n0_ref, p1_ref, r1_ref):
    k = PEER_TOPK

    def head(h, carry):
        col0 = pl.multiple_of(h * (2 * PEER_HALF), LANES)
        col1 = pl.multiple_of(col0 + PEER_HALF, LANES)
        s0 = _dot_nt(keys_ref[2 * h], q_ref[:, pl.ds(col0, PEER_HALF)])
        s1 = _dot_nt(keys_ref[2 * h + 1], q_ref[:, pl.ds(col1, PEER_HALF)])

        def route(exact):
            rank0, v0 = _top_rows(s0, k, exact)
            rank1, v1 = _top_rows(s1, k, exact)
            n_rank, z, n_sel = (_pair_counts_exact if exact else _pair_counts_fast)(v0, v1, k)
            n0 = jnp.zeros_like(s0)
            for r in range(k):
                n0 = jnp.where(rank0 == float(r), n_rank[r:r + 1], n0)
            p0_ref[h] = jnp.where(rank0 < float(k), jnp.exp(s0 - v0[0:1]), 0.0) / z
            n0_ref[h] = n0
            p1_ref[h] = jnp.where(rank1 < float(k), jnp.exp(s1 - v1[0:1]), 0.0).astype(p1_ref.dtype)
            r1_ref[h] = rank1.astype(r1_ref.dtype)
            return _count_ranked(rank0, k) + _count_ranked(rank1, k) + n_sel

        n_marked = route(False)

        @pl.when(jnp.max(n_marked) > 3.0 * k)
        def _():
            route(True)

        return carry

    lax.fori_loop(0, PEER_HEADS, head, 0)


def _router(q, keys):
    t = q.shape[0]
    tb = ROUTER_TOKENS
    blk = pl.BlockSpec((PEER_HEADS, PEER_N_KEYS, tb), lambda i: (0, 0, i))
    f32s = jax.ShapeDtypeStruct((PEER_HEADS, PEER_N_KEYS, t), F32)
    bf16s = jax.ShapeDtypeStruct((PEER_HEADS, PEER_N_KEYS, t), BF16)
    return pl.pallas_call(
        _router_kernel, grid=(t // tb,),
        in_specs=[pl.BlockSpec((tb, q.shape[1]), lambda i: (i, 0)),
                  pl.BlockSpec(keys.shape, lambda i: (0, 0, 0))],
        out_specs=[blk, blk, blk, blk], out_shape=[f32s, f32s, bf16s, bf16s],
        compiler_params=_cparams("parallel"), name="peer_router",
    )(q, keys)


EXPERT_TOKENS = 512
EXPERT_CHUNK = 1024
ROWS_PER_CHUNK = EXPERT_CHUNK // PEER_N_KEYS


MM1_SLICES = 2
PACK = 16


def _gate_tile(pre_ref, p0_ref, n0_ref, p1_ref, r1_ref, h_ref, tile, cache):
    tb = pre_ref.shape[1]
    ii, rt = divmod(tile, PEER_N_KEYS // PACK)
    if ii not in cache:
        cache.clear()
        cache[ii] = (
            [jnp.broadcast_to(p0_ref[h, ii:ii + 1, :], (PACK, tb)).astype(BF16)
             for h in range(PEER_HEADS)],
            [jnp.broadcast_to(n0_ref[h, ii:ii + 1, :], (PACK, tb)).astype(BF16)
             for h in range(PEER_HEADS)])
    gates, cnts = cache[ii]
    js = slice(rt * PACK, (rt + 1) * PACK)
    w = None
    for h in range(PEER_HEADS):
        term = jnp.where(r1_ref[h, js, :] < cnts[h], p1_ref[h, js, :],
                         jnp.zeros((), BF16)) * gates[h]
        w = term if w is None else w + term
    rows = slice(tile * PACK, (tile + 1) * PACK)
    pre = pre_ref[rows, :]
    act = 0.5 * pre * (1.0 + lax.erf(pre * np.sqrt(0.5).astype(np.float32)))
    h_ref[rows, :] = act.astype(BF16) * w


def _expert_kernel(*refs, with_norm):
    xn_ref, x_ref, u_ref, vt_ref, p0_ref, n0_ref, p1_ref, r1_ref = refs[:8]
    g_ref = refs[8] if with_norm else None
    o_ref, acc_ref, pre_ref, h_ref = refs[-4:]
    c = pl.program_id(1)

    @pl.when(c == 0)
    def _():
        acc_ref[...] = jnp.zeros_like(acc_ref)

    cache = {}
    n_tiles = EXPERT_CHUNK // PACK
    for k in range(MM1_SLICES):
        rows = slice(k * EXPERT_CHUNK // MM1_SLICES, (k + 1) * EXPERT_CHUNK // MM1_SLICES)
        pre_ref[rows, :] = _dot_nt(u_ref[rows, :], xn_ref[...])
        for tile in range(k * n_tiles // MM1_SLICES, (k + 1) * n_tiles // MM1_SLICES):
            _gate_tile(pre_ref, p0_ref, n0_ref, p1_ref, r1_ref, h_ref, tile, cache)
    acc_ref[...] += jnp.dot(vt_ref[...], h_ref[...], preferred_element_type=F32)

    @pl.when(c == pl.num_programs(1) - 1)
    def _():
        y = x_ref[...] + acc_ref[...].T
        if with_norm:
            ms = jnp.mean(y * y, axis=-1, keepdims=True)
            y = y * lax.rsqrt(ms + EPS) * g_ref[...]
        o_ref[...] = y


def _experts(xn, x, u, vt, p0, n0, p1, r1, out_gain=None):
    t, d = x.shape
    tb, ec = EXPERT_TOKENS, EXPERT_CHUNK
    small = pl.BlockSpec((PEER_HEADS, ROWS_PER_CHUNK, tb), lambda i, c: (0, c, i))
    full = pl.BlockSpec((PEER_HEADS, PEER_N_KEYS, tb), lambda i, c: (0, 0, i))
    with_norm = out_gain is not None
    gain = ([pl.BlockSpec((1, d), lambda i, c: (0, 0))], [out_gain.reshape(1, d)]) if with_norm else ([], [])
    return pl.pallas_call(
        functools.partial(_expert_kernel, with_norm=with_norm), grid=(t // tb, PEER_EXPERTS // ec),
        in_specs=[pl.BlockSpec((tb, d), lambda i, c: (i, 0)),
                  pl.BlockSpec((tb, d), lambda i, c: (i, 0)),
                  pl.BlockSpec((ec, d), lambda i, c: (c, 0)),
                  pl.BlockSpec((None, d, ec), lambda i, c: (c, 0, 0)),
                  small, small, full, full] + gain[0],
        out_specs=pl.BlockSpec((tb, d), lambda i, c: (i, 0)),
        out_shape=jax.ShapeDtypeStruct((t, d), F32),
        scratch_shapes=[pltpu.VMEM((d, tb), F32), pltpu.VMEM((ec, tb), F32),
                        pltpu.VMEM((ec, tb), BF16)],
        compiler_params=_cparams("parallel", "arbitrary"), name="peer_experts",
    )(xn, x, u, vt, p0, n0, p1, r1, *gain[1])


def _peer(x, g, wq, keys, u, v, out_gain=None):
    xq, xn = _proj(x, g, wq.astype(MXU_DTYPE), k_cols=x.shape[1], tm=512, tn=wq.shape[1],
                   with_xn=True, out_dtype=MXU_DTYPE)
    keys2 = keys.reshape(2 * PEER_HEADS, PEER_N_KEYS, PEER_HALF).astype(MXU_DTYPE)
    p0, n0, p1, r1 = _router(xq, keys2)
    nc = PEER_EXPERTS // EXPERT_CHUNK
    vt = v.astype(MXU_DTYPE).reshape(nc, EXPERT_CHUNK, -1).transpose(0, 2, 1)
    return _experts(xn, x, u.astype(MXU_DTYPE), vt, p0, n0, p1, r1, out_gain)


def kernel(x, positions, norm_mix0, w_in0, sinks0, conv_w0, w_out0, norm_ffn0, peer_wq0, peer_keys0,
           peer_u0, peer_v0, norm_mix1, w_in1, g_qa1, w_uq1, w_out1, norm_ffn1, peer_wq1,
           peer_keys1, peer_u1, peer_v1, norm_final):
    b, s, d = x.shape
    t = b * s
    xf = x.reshape(t, d)
    pos = positions.reshape(t)
    tab_a = _rope_table(pos, ((0, 64), (64, 64)), 8)
    tab_b = _rope_table(pos, ((0, 128),), 16)
    tab_c = _rope_table(pos, ((0, 64),), 8)

    n0 = w_in0.shape[1]
    n_rope = (A_Q_COLS + A_KV_COLS) // LANES
    cfg0 = (0,) * n_rope + (-1,) * (n0 // LANES - n_rope)
    p0 = _proj(xf, norm_mix0, w_in0.astype(MXU_DTYPE), k_cols=d, tm=256, tn=n0,
               tabs=(tab_a,), halves=(8,), groups=(((0, 1), cfg0),))
    a_out = _swa(p0, sinks0, b, s)
    b_out = _conv(p0, conv_w0, s)
    w_o = w_out0.astype(MXU_DTYPE)
    x1 = _mm_res([a_out, b_out], [w_o[:A_Q_COLS], w_o[A_Q_COLS:]], xf, tm=512, tn=d)
    x2 = _peer(x1, norm_ffn0, peer_wq0, peer_keys0, peer_u0, peer_v0)

    w1 = jnp.pad(w_in1, ((0, 0), (0, IN1_COLS_PADDED - w_in1.shape[1]))).astype(MXU_DTYPE)
    k_chunk = C_Q_RANK // LANES
    cfg1 = tuple({k_chunk: 0, k_chunk + 2: 1}.get(c, -1) for c in range(IN1_COLS_PADDED // LANES))
    p1 = _proj(x2, norm_mix1, w1, k_cols=d, tm=512, tn=IN1_COLS_PADDED,
               tabs=(tab_b, tab_c), halves=(16, 8), groups=(((0, 1), cfg1),))
    n_q = C_HEADS * C_HEAD_DIM // LANES
    n_iq = IDX_HEADS * IDX_HEAD_DIM // LANES
    cfg_u = (0,) * n_q + (1,) * n_iq
    u = _proj(p1, g_qa1, w_uq1.astype(MXU_DTYPE), k_cols=C_Q_RANK, tm=256,
              tn=(n_q + n_iq) * LANES, tabs=(tab_b, tab_a), halves=(16, 8),
              groups=(((0, 1), cfg_u),), scales=(C_HEAD_DIM ** -0.5,) * n_q + (1.0,) * n_iq,
              out_dtype=MXU_DTYPE)
    o = _dsa(u, p1, b, s)
    x3 = _mm_res([o], [w_out1.astype(MXU_DTYPE)], x2, tm=512, tn=d)
    out = _peer(x3, norm_ffn1, peer_wq1, peer_keys1, peer_u1, peer_v1, out_gain=norm_final)
    return out.reshape(b, s, d)
```

```python
import functools

import numpy as np
import jax
import jax.numpy as jnp
from jax import lax
from jax.experimental import pallas as pl
from jax.experimental.pallas import tpu as pltpu

F32 = jnp.float32
BF16 = jnp.bfloat16
MXU_DTYPE = jnp.bfloat16

LANES = 128
EPS = 1e-5
ROPE_THETA = 500000.0
BLOCK = 128
NEG_INF = -1e30
INT_MIN = -(2 ** 31)

SWA_WINDOW = 128
A_Q_HEADS, A_KV_HEADS, A_HEAD_DIM = 16, 4, 64
A_Q_COLS = A_Q_HEADS * A_HEAD_DIM
A_KV_COLS = A_KV_HEADS * A_HEAD_DIM
B_WIDTH = 1024
C_Q_RANK, C_HEADS, C_HEAD_DIM = 512, 16, 128
IDX_HEADS, IDX_HEAD_DIM, INDEX_TOPK = 16, 64, 256
IN1_COLS_PADDED = 896
PEER_HEADS, PEER_N_KEYS, PEER_TOPK, PEER_HALF = 8, 128, 16, 128
PEER_EXPERTS = PEER_N_KEYS * PEER_N_KEYS

VMEM_LIMIT = 56 * 1024 * 1024


def _cparams(*sem):
    return pltpu.CompilerParams(dimension_semantics=sem, vmem_limit_bytes=VMEM_LIMIT)


def _resident(n_tiles):
    return pl.Buffered(1) if n_tiles == 1 else None


def _col_tiles(w, tn):
    k, n = w.shape
    return w.reshape(k, n // tn, tn).transpose(1, 0, 2)


def _dot(a, b):
    return jnp.dot(a.astype(MXU_DTYPE), b.astype(MXU_DTYPE), preferred_element_type=F32)


def _dot_nt(a, b):
    return lax.dot_general(a.astype(MXU_DTYPE), b.astype(MXU_DTYPE),
                           (((1,), (1,)), ((), ())), preferred_element_type=F32)


def _rope_table(pos, heads, half):
    rot = 2 * half
    inv = ROPE_THETA ** (-jnp.arange(half, dtype=F32) * 2.0 / rot)
    freq = np.zeros(LANES, np.int32)
    rotated = np.zeros(LANES, bool)
    upper = np.zeros(LANES, bool)
    for lane0, dh in heads:
        assert dh // 8 == half
        freq[lane0:lane0 + rot] = np.arange(rot) % half
        rotated[lane0:lane0 + rot] = True
        upper[lane0 + half:lane0 + rot] = True
    ang = pos.astype(F32)[:, None] * inv[None, :]
    cos, sin = jnp.cos(ang), jnp.sin(ang)
    onehot = (np.arange(half)[:, None] == freq[None, :]) & rotated[None, :]
    spread = lambda v, m: jnp.dot(v, jnp.asarray(onehot & m[None, :], F32),
                                  precision=lax.Precision.HIGHEST)
    c = spread(cos, rotated) + jnp.asarray(~rotated, F32)
    return jnp.stack([c, spread(sin, upper), spread(-sin, rotated & ~upper)])


def _rope_chunk(xc, tab_ref, half):
    return (xc * tab_ref[0] + pltpu.roll(xc, half, 1) * tab_ref[1]
            + pltpu.roll(xc, LANES - half, 1) * tab_ref[2])


def _proj_kernel(*refs, n_tabs, halves, groups, scales, tn, with_xn):
    x_ref, g_ref, w_ref = refs[:3]
    tab_refs = refs[3:3 + n_tabs]
    o_ref = refs[3 + n_tabs]
    xn_out = refs[4 + n_tabs] if with_xn else None
    xn_ref = refs[-1]
    j = pl.program_id(1)

    @pl.when(j == 0)
    def _():
        x = x_ref[...]
        ms = jnp.mean(x * x, axis=-1, keepdims=True)
        xn = (x * lax.rsqrt(ms + EPS) * g_ref[...]).astype(MXU_DTYPE)
        xn_ref[...] = xn
        if with_xn:
            xn_out[...] = xn

    acc = jnp.dot(xn_ref[...], w_ref[...], preferred_element_type=F32)

    for (j_lo, j_hi), cfg in groups:
        @pl.when((j >= j_lo) & (j < j_hi))
        def _(cfg=cfg):
            for c in range(tn // LANES):
                xc = acc[:, c * LANES:(c + 1) * LANES]
                if cfg[c] >= 0:
                    xc = _rope_chunk(xc, tab_refs[cfg[c]], halves[cfg[c]])
                if scales[c] != 1.0:
                    xc = xc * scales[c]
                o_ref[:, c * LANES:(c + 1) * LANES] = xc.astype(o_ref.dtype)


def _proj(x, g, w, *, k_cols, tm, tn, tabs=(), halves=(), groups=None, scales=None, with_xn=False,
          out_dtype=F32):
    t = x.shape[0]
    n = w.shape[1]
    nj = n // tn
    if groups is None:
        groups = (((0, nj), (-1,) * (tn // LANES)),)
    if scales is None:
        scales = (1.0,) * (tn // LANES)
    kern = functools.partial(_proj_kernel, n_tabs=len(tabs), halves=tuple(halves),
                             groups=tuple(groups), scales=tuple(scales), tn=tn, with_xn=with_xn)
    in_specs = [pl.BlockSpec((tm, k_cols), lambda i, j: (i, 0)),
                pl.BlockSpec((1, k_cols), lambda i, j: (0, 0)),
                pl.BlockSpec((None, k_cols, tn), lambda i, j: (j, 0, 0),
                             pipeline_mode=_resident(nj))]
    in_specs += [pl.BlockSpec((3, tm, LANES), lambda i, j: (0, i, 0)) for _ in tabs]
    out_shape = [jax.ShapeDtypeStruct((t, n), out_dtype)]
    out_specs = [pl.BlockSpec((tm, tn), lambda i, j: (i, j))]
    if with_xn:
        out_shape.append(jax.ShapeDtypeStruct((t, k_cols), MXU_DTYPE))
        out_specs.append(pl.BlockSpec((tm, k_cols), lambda i, j: (i, 0)))
    res = pl.pallas_call(
        kern, grid=(t // tm, nj), in_specs=in_specs, out_specs=out_specs, out_shape=out_shape,
        scratch_shapes=[pltpu.VMEM((tm, k_cols), MXU_DTYPE)],
        compiler_params=_cparams("parallel", "arbitrary"), name="norm_proj",
    )(x, g.reshape(1, k_cols), _col_tiles(w, tn), *tabs)
    return res if with_xn else res[0]


def _mm_res_kernel(*refs, n_lhs):
    a_refs = refs[:n_lhs]
    w_refs = refs[n_lhs:2 * n_lhs]
    x_ref = refs[2 * n_lhs]
    o_ref = refs[2 * n_lhs + 1]
    ab_refs = refs[2 * n_lhs + 2:]

    @pl.when(pl.program_id(1) == 0)
    def _():
        for a_ref, ab_ref in zip(a_refs, ab_refs):
            ab_ref[...] = a_ref[...].astype(MXU_DTYPE)

    acc = x_ref[...]
    for ab_ref, w_ref in zip(ab_refs, w_refs):
        acc = acc + jnp.dot(ab_ref[...], w_ref[...], preferred_element_type=F32)
    o_ref[...] = acc


def _mm_res(lhs, ws, x, *, tm, tn):
    t, d = x.shape
    n_lhs = len(lhs)
    in_specs = [pl.BlockSpec((tm, a.shape[1]), lambda i, j: (i, 0)) for a in lhs]
    in_specs += [pl.BlockSpec((None, w.shape[0], tn), lambda i, j: (j, 0, 0),
                              pipeline_mode=_resident(d // tn)) for w in ws]
    in_specs += [pl.BlockSpec((tm, tn), lambda i, j: (i, j))]
    return pl.pallas_call(
        functools.partial(_mm_res_kernel, n_lhs=n_lhs), grid=(t // tm, d // tn),
        in_specs=in_specs, out_specs=pl.BlockSpec((tm, tn), lambda i, j: (i, j)),
        out_shape=jax.ShapeDtypeStruct((t, d), F32),
        scratch_shapes=[pltpu.VMEM((tm, a.shape[1]), MXU_DTYPE) for a in lhs],
        compiler_params=_cparams("parallel", "arbitrary"), name="out_proj",
    )(*lhs, *[_col_tiles(w, tn) for w in ws], x)


def _swa_kernel(sink_ref, q_ref, kp_ref, kc_ref, vp_ref, vc_ref, o_ref):
    n = pl.program_id(1)
    g_sz = A_Q_HEADS // A_KV_HEADS
    dh = A_HEAD_DIM
    q = q_ref[...]
    k2 = jnp.concatenate([kp_ref[...], kc_ref[...]], axis=0).astype(MXU_DTYPE)
    v2 = jnp.concatenate([vp_ref[...], vc_ref[...]], axis=0).astype(MXU_DTYPE)
    qi = lax.broadcasted_iota(jnp.int32, (BLOCK, 2 * BLOCK), 0)
    kj = lax.broadcasted_iota(jnp.int32, (BLOCK, 2 * BLOCK), 1)
    diff = qi + BLOCK - kj
    valid = (diff >= 0) & (diff < SWA_WINDOW) & ((kj >= BLOCK) | (n > 0))
    scale = dh ** -0.5
    for h in range(A_KV_HEADS):
        kh = k2[:, h * dh:(h + 1) * dh]
        vh = v2[:, h * dh:(h + 1) * dh]
        for g in range(g_sz):
            hq = h * g_sz + g
            s = _dot_nt(q[:, hq * dh:(hq + 1) * dh], kh) * scale
            s = jnp.where(valid, s, NEG_INF)
            sink = sink_ref[hq]
            m = jnp.maximum(jnp.max(s, axis=-1, keepdims=True), sink)
            p = jnp.exp(s - m)
            denom = jnp.sum(p, axis=-1, keepdims=True) + jnp.exp(sink - m)
            o_ref[:, hq * dh:(hq + 1) * dh] = (_dot(p, vh) / denom).astype(o_ref.dtype)


def _swa(p0, sinks, b, s):
    nb = s // BLOCK
    kv_w = A_KV_COLS
    k_blk = A_Q_COLS // kv_w
    v_blk = k_blk + 1
    cur = lambda bi, n: bi * nb + n
    prev = lambda bi, n: bi * nb + jnp.maximum(n - 1, 0)
    return pl.pallas_call(
        _swa_kernel, grid=(b, nb),
        in_specs=[pl.BlockSpec(memory_space=pltpu.SMEM),
                  pl.BlockSpec((BLOCK, A_Q_COLS), lambda bi, n: (cur(bi, n), 0)),
                  pl.BlockSpec((BLOCK, kv_w), lambda bi, n: (prev(bi, n), k_blk)),
                  pl.BlockSpec((BLOCK, kv_w), lambda bi, n: (cur(bi, n), k_blk)),
                  pl.BlockSpec((BLOCK, kv_w), lambda bi, n: (prev(bi, n), v_blk)),
                  pl.BlockSpec((BLOCK, kv_w), lambda bi, n: (cur(bi, n), v_blk))],
        out_specs=pl.BlockSpec((BLOCK, A_Q_COLS), lambda bi, n: (cur(bi, n), 0)),
        out_shape=jax.ShapeDtypeStruct((b * s, A_Q_COLS), MXU_DTYPE),
        compiler_params=_cparams("parallel", "arbitrary"), name="swa",
    )(sinks, p0, p0, p0, p0, p0)


CONV_ROWS = 256
CONV_COLS = 512
HALO = 8


def _conv_kernel(bg_ref, cg_ref, hx_ref, cgh_ref, hxh_ref, w_ref, o_ref, *, tiles_per_seq):
    i = pl.program_id(0)
    z = cg_ref[...] * hx_ref[...]
    first = (i % tiles_per_seq) == 0
    zh = jnp.where(first, 0.0, cgh_ref[...] * hxh_ref[...])
    row = lax.broadcasted_iota(jnp.int32, z.shape, 0)
    z1 = pltpu.roll(z, 1, 0)
    z1 = jnp.where(row == 0, zh[HALO - 1:HALO], z1)
    z2 = pltpu.roll(z, 2, 0)
    z2 = jnp.where(row == 0, zh[HALO - 2:HALO - 1], jnp.where(row == 1, zh[HALO - 1:HALO], z2))
    w = w_ref[...]
    o_ref[...] = (bg_ref[...] * (w[0:1] * z2 + w[1:2] * z1 + w[2:3] * z)).astype(o_ref.dtype)


def _conv(p0, conv_w, s):
    t = p0.shape[0]
    col0 = A_Q_COLS + 2 * A_KV_COLS
    bg_blk, cg_blk, hx_blk = (col0 // CONV_COLS, (col0 + B_WIDTH) // CONV_COLS,
                              (col0 + 2 * B_WIDTH) // CONV_COLS)
    rh = CONV_ROWS // HALO
    halo_row = lambda i: jnp.maximum(i * rh - 1, 0)
    return pl.pallas_call(
        functools.partial(_conv_kernel, tiles_per_seq=s // CONV_ROWS),
        grid=(t // CONV_ROWS, B_WIDTH // CONV_COLS),
        in_specs=[pl.BlockSpec((CONV_ROWS, CONV_COLS), lambda i, j: (i, bg_blk + j)),
                  pl.BlockSpec((CONV_ROWS, CONV_COLS), lambda i, j: (i, cg_blk + j)),
                  pl.BlockSpec((CONV_ROWS, CONV_COLS), lambda i, j: (i, hx_blk + j)),
                  pl.BlockSpec((HALO, CONV_COLS), lambda i, j: (halo_row(i), cg_blk + j)),
                  pl.BlockSpec((HALO, CONV_COLS), lambda i, j: (halo_row(i), hx_blk + j)),
                  pl.BlockSpec((3, CONV_COLS), lambda i, j: (0, j))],
        out_specs=pl.BlockSpec((CONV_ROWS, CONV_COLS), lambda i, j: (i, j)),
        out_shape=jax.ShapeDtypeStruct((t, B_WIDTH), MXU_DTYPE),
        compiler_params=_cparams("parallel", "parallel"), name="gated_conv",
    )(p0, p0, p0, p0, p0, conv_w)


DSA_WIDTHS = 4
COUNT_CHAINS = 8


def _dsa_block(qa_ref, qb_ref, iq_ref, iwq_ref, k_ref, v_ref, ikw_ref, o_ref, key_ref, half_ref,
               n, w, k_sel):
    ik = ikw_ref[0:w, 0:IDX_HEAD_DIM].astype(MXU_DTYPE)
    iw_t = (iwq_ref[...] * ((IDX_HEADS ** -0.5) * (IDX_HEAD_DIM ** -0.5))).T
    iq = iq_ref[...]
    heads_per_chunk = LANES // IDX_HEAD_DIM
    score = jnp.zeros((w, BLOCK), F32)
    for c in range(IDX_HEADS // heads_per_chunk):
        chunk = iq[:, c * LANES:(c + 1) * LANES]
        rhs = jnp.concatenate([chunk[:, hh * IDX_HEAD_DIM:(hh + 1) * IDX_HEAD_DIM]
                               for hh in range(heads_per_chunk)], axis=0)
        lg = _dot_nt(ik, rhs)
        for hh in range(heads_per_chunk):
            row = IDX_HEAD_DIM + c * heads_per_chunk + hh
            score = score + iw_t[row:row + 1, :] * jnp.maximum(lg[:, hh * BLOCK:(hh + 1) * BLOCK], 0.0)
    score = score + 0.0
    kpos = lax.broadcasted_iota(jnp.int32, (w, BLOCK), 0)
    qpos = n * BLOCK + lax.broadcasted_iota(jnp.int32, (w, BLOCK), 1)
    bits = pltpu.bitcast(score, jnp.int32)
    key = jnp.where(bits < 0, bits ^ jnp.int32(0x7FFFFFFF), bits)
    key = jnp.where(kpos <= qpos, key, INT_MIN)
    key_ref[0:w, :] = key
    half_ref[0:w, :] = jnp.right_shift(key, 16).astype(jnp.int16)

    def search16():
        def count_ge(t):
            t16 = t.astype(jnp.int16)
            accs = [None] * COUNT_CHAINS
            for c in range(w // PACK):
                one = jnp.where(half_ref[c * PACK:(c + 1) * PACK, :] >= t16,
                                jnp.int16(1), jnp.int16(0))
                a = c % COUNT_CHAINS
                accs[a] = one if accs[a] is None else accs[a] + one
            acc = functools.reduce(lambda x, y: x + y, [a for a in accs if a is not None])
            return jnp.sum(acc.astype(F32), axis=0, keepdims=True)

        kf = float(k_sel)
        t0 = jnp.where(count_ge(jnp.zeros((1, BLOCK), jnp.int32)) >= kf,
                       jnp.int32(0), jnp.int32(-2 ** 15))

        def bit_step(it, t):
            cand = t + jnp.left_shift(jnp.int32(1), jnp.int32(14) - it)
            return jnp.where(count_ge(cand) >= kf, cand, t)

        return lax.fori_loop(0, 15, bit_step, t0)

    t_hi = search16()
    hi = half_ref[0:w, :].astype(jnp.int32)
    lo = jnp.bitwise_and(key_ref[0:w, :], 0xFFFF) - 2 ** 15
    half_ref[0:w, :] = jnp.where(hi > t_hi, 2 ** 15 - 1,
                                 jnp.where(hi < t_hi, -2 ** 15, lo)).astype(jnp.int16)
    t_lo = search16()
    thr = jnp.maximum(t_hi * 2 ** 16 + (t_lo + 2 ** 15), INT_MIN + 1)
    bias = jnp.where(key_ref[0:w, :] >= thr, 0.0, NEG_INF).T

    kk = k_ref[0:w, :].astype(MXU_DTYPE)
    lane = lax.broadcasted_iota(jnp.int32, (w, LANES), 1)
    v_aug = jnp.concatenate([v_ref[0:w, :].astype(MXU_DTYPE),
                             jnp.where(lane == 0, 1.0, 0.0).astype(MXU_DTYPE)], axis=1)
    for h in range(C_HEADS):
        q_r = qa_ref if h < C_HEADS // 2 else qb_ref
        hh = h % (C_HEADS // 2)
        s = _dot_nt(q_r[:, hh * C_HEAD_DIM:(hh + 1) * C_HEAD_DIM], kk) + bias
        p = jnp.exp(s - jnp.max(s, axis=-1, keepdims=True))
        pv = _dot(p, v_aug)
        o_ref[:, h * C_HEAD_DIM:(h + 1) * C_HEAD_DIM] = (
            pv[:, :C_HEAD_DIM] / pv[:, C_HEAD_DIM:C_HEAD_DIM + 1]).astype(o_ref.dtype)


def _dsa_kernel(qa_ref, qb_ref, iq_ref, iwq_ref, k_ref, v_ref, ikw_ref, o_ref, key_ref, half_ref,
                *, k_sel):
    n = pl.program_id(1)
    chunk = k_ref.shape[0] // DSA_WIDTHS
    n_chunks = (n * BLOCK + BLOCK - 1) // chunk + 1
    for wi in range(1, DSA_WIDTHS + 1):
        if (wi * chunk) % BLOCK:
            continue

        @pl.when(n_chunks == wi)
        def _(wi=wi):
            _dsa_block(qa_ref, qb_ref, iq_ref, iwq_ref, k_ref, v_ref, ikw_ref, o_ref, key_ref,
                       half_ref, n, wi * chunk, k_sel)


def _dsa(u, p1, b, s):
    nb = s // BLOCK
    k_sel = min(INDEX_TOPK, s // 4)
    half_q = C_HEADS * C_HEAD_DIM // 2
    k_blk = C_Q_RANK // LANES
    row = lambda bi, n: bi * nb + n
    return pl.pallas_call(
        functools.partial(_dsa_kernel, k_sel=k_sel), grid=(b, nb),
        in_specs=[pl.BlockSpec((BLOCK, half_q), lambda bi, n: (row(bi, n), 0)),
                  pl.BlockSpec((BLOCK, half_q), lambda bi, n: (row(bi, n), 1)),
                  pl.BlockSpec((BLOCK, half_q), lambda bi, n: (row(bi, n), 2)),
                  pl.BlockSpec((BLOCK, LANES), lambda bi, n: (row(bi, n), k_blk + 2)),
                  pl.BlockSpec((s, LANES), lambda bi, n: (bi, k_blk)),
                  pl.BlockSpec((s, LANES), lambda bi, n: (bi, k_blk + 1)),
                  pl.BlockSpec((s, LANES), lambda bi, n: (bi, k_blk + 2))],
        out_specs=pl.BlockSpec((BLOCK, C_HEADS * C_HEAD_DIM), lambda bi, n: (row(bi, n), 0)),
        out_shape=jax.ShapeDtypeStruct((b * s, C_HEADS * C_HEAD_DIM), MXU_DTYPE),
        scratch_shapes=[pltpu.VMEM((s, BLOCK), jnp.int32), pltpu.VMEM((s, BLOCK), jnp.int16)],
        compiler_params=_cparams("parallel", "arbitrary"), name="dsa",
    )(u, u, u, p1, p1, p1, p1)


ROUTER_TOKENS = 512
N_FULL_K0 = 8
SUB = 8


def _top_rows(work, k, exact):
    rows, cols = work.shape
    vrow = lax.broadcasted_iota(jnp.int32, (k, cols), 0)
    vals = jnp.zeros((k, cols), F32)
    if exact:
        row = lax.broadcasted_iota(jnp.int32, (rows, cols), 0).astype(F32)
        rank = jnp.full((rows, cols), float(k), F32)
    for r in range(k):
        m = jnp.max(work, axis=0, keepdims=True)
        hit = work == m
        if exact:
            first = jnp.min(jnp.where(hit, row, float(rows)), axis=0, keepdims=True)
            hit = row == first
            rank = jnp.where(hit, float(r), rank)
            work = jnp.where(hit, -jnp.inf, work)
        else:
            work = jnp.where(hit, -(2.0 ** 127) * (1.0 + r / 16.0), work)
        vals = jnp.where(vrow == r, m, vals)
    if not exact:
        assert k == 16
        rank = jnp.where(work <= -(2.0 ** 127), work * -(2.0 ** -123) - 16.0, float(k))
    return rank, vals


def _count_ranked(rank, k):
    return jnp.sum(jnp.where(rank < float(k), 1.0, 0.0), axis=0, keepdims=True)


def _pair_counts_exact(v0, v1, k):
    cand = jnp.concatenate([v0[k0:k0 + 1] + v1 for k0 in range(N_FULL_K0)]
                           + [v0[N_FULL_K0:] + v1[0:1]], axis=0)
    rank_c, _ = _top_rows(cand, k, True)
    sel = rank_c < float(k)
    cnt = jnp.where(sel, 1.0, 0.0)
    n_rank = jnp.concatenate(
        [jnp.sum(cnt[k0 * k:(k0 + 1) * k], axis=0, keepdims=True) for k0 in range(N_FULL_K0)]
        + [cnt[N_FULL_K0 * k:]], axis=0)
    z = jnp.sum(jnp.where(sel, jnp.exp(cand - cand[0:1]), 0.0), axis=0, keepdims=True)
    return n_rank, z, jnp.sum(cnt, axis=0, keepdims=True)


def _pair_counts_fast(v0, v1, k):
    cols = v0.shape[1]
    row = lax.broadcasted_iota(jnp.int32, (SUB, cols), 0)
    tiles = []
    for k1 in range(SUB):
        n_valid = min(k // (k1 + 1), SUB)
        tiles.append(jnp.where(row < n_valid, v0[0:SUB] + v1[k1:k1 + 1], -jnp.inf))
    tiles.append(v0[0:1] + v1[SUB:])
    tiles.append(v0[SUB:] + v1[0:1])
    cand = jnp.concatenate(tiles, axis=0)
    rank_c, _ = _top_rows(cand, k, False)
    sel = rank_c < float(k)
    cnt = jnp.where(sel, 1.0, 0.0)
    low = cnt[0:SUB]
    for k1 in range(1, SUB):
        low = low + cnt[k1 * SUB:(k1 + 1) * SUB]
    tail = jnp.sum(cnt[SUB * SUB:SUB * SUB + SUB], axis=0, keepdims=True)
    low = low + jnp.where(row == 0, tail, 0.0)
    n_rank = jnp.concatenate([low, cnt[SUB * SUB + SUB:]], axis=0)
    z = jnp.sum(jnp.where(sel, jnp.exp(cand - (v0[0:1] + v1[0:1])), 0.0), axis=0, keepdims=True)
    return n_rank, z, jnp.sum(cnt, axis=0, keepdims=True)


def _router_kernel(q_ref, keys_ref, p0_ref, n0_ref, p1_ref, r1_ref):
    k = PEER_TOPK

    def head(h, carry):
        col0 = pl.multiple_of(h * (2 * PEER_HALF), LANES)
        col1 = pl.multiple_of(col0 + PEER_HALF, LANES)
        s0 = _dot_nt(keys_ref[2 * h], q_ref[:, pl.ds(col0, PEER_HALF)])
        s1 = _dot_nt(keys_ref[2 * h + 1], q_ref[:, pl.ds(col1, PEER_HALF)])

        def route(exact):
            rank0, v0 = _top_rows(s0, k, exact)
            rank1, v1 = _top_rows(s1, k, exact)
            n_rank, z, n_sel = (_pair_counts_exact if exact else _pair_counts_fast)(v0, v1, k)
            n0 = jnp.zeros_like(s0)
            for r in range(k):
                n0 = jnp.where(rank0 == float(r), n_rank[r:r + 1], n0)
            p0_ref[h] = jnp.where(rank0 < float(k), jnp.exp(s0 - v0[0:1]), 0.0) / z
            n0_ref[h] = n0
            p1_ref[h] = jnp.where(rank1 < float(k), jnp.exp(s1 - v1[0:1]), 0.0).astype(p1_ref.dtype)
            r1_ref[h] = rank1.astype(r1_ref.dtype)
            return _count_ranked(rank0, k) + _count_ranked(rank1, k) + n_sel

        n_marked = route(False)

        @pl.when(jnp.max(n_marked) > 3.0 * k)
        def _():
            route(True)

        return carry

    lax.fori_loop(0, PEER_HEADS, head, 0)


def _router(q, keys):
    t = q.shape[0]
    tb = ROUTER_TOKENS
    blk = pl.BlockSpec((PEER_HEADS, PEER_N_KEYS, tb), lambda i: (0, 0, i))
    f32s = jax.ShapeDtypeStruct((PEER_HEADS, PEER_N_KEYS, t), F32)
    bf16s = jax.ShapeDtypeStruct((PEER_HEADS, PEER_N_KEYS, t), BF16)
    return pl.pallas_call(
        _router_kernel, grid=(t // tb,),
        in_specs=[pl.BlockSpec((tb, q.shape[1]), lambda i: (i, 0)),
                  pl.BlockSpec(keys.shape, lambda i: (0, 0, 0))],
        out_specs=[blk, blk, blk, blk], out_shape=[f32s, f32s, bf16s, bf16s],
        compiler_params=_cparams("parallel"), name="peer_router",
    )(q, keys)


EXPERT_TOKENS = 512
EXPERT_CHUNK = 1024
ROWS_PER_CHUNK = EXPERT_CHUNK // PEER_N_KEYS


MM1_SLICES = 2
PACK = 16


def _gate_tile(pre_ref, p0_ref, n0_ref, p1_ref, r1_ref, h_ref, tile, cache):
    tb = pre_ref.shape[1]
    ii, rt = divmod(tile, PEER_N_KEYS // PACK)
    if ii not in cache:
        cache.clear()
        cache[ii] = (
            [jnp.broadcast_to(p0_ref[h, ii:ii + 1, :], (PACK, tb)).astype(BF16)
             for h in range(PEER_HEADS)],
            [jnp.broadcast_to(n0_ref[h, ii:ii + 1, :], (PACK, tb)).astype(BF16)
             for h in range(PEER_HEADS)])
    gates, cnts = cache[ii]
    js = slice(rt * PACK, (rt + 1) * PACK)
    w = None
    for h in range(PEER_HEADS):
        term = jnp.where(r1_ref[h, js, :] < cnts[h], p1_ref[h, js, :],
                         jnp.zeros((), BF16)) * gates[h]
        w = term if w is None else w + term
    rows = slice(tile * PACK, (tile + 1) * PACK)
    pre = pre_ref[rows, :]
    act = 0.5 * pre * (1.0 + lax.erf(pre * np.sqrt(0.5).astype(np.float32)))
    h_ref[rows, :] = act.astype(BF16) * w


def _expert_kernel(*refs, with_norm):
    xn_ref, x_ref, u_ref, vt_ref, p0_ref, n0_ref, p1_ref, r1_ref = refs[:8]
    g_ref = refs[8] if with_norm else None
    o_ref, acc_ref, pre_ref, h_ref = refs[-4:]
    c = pl.program_id(1)

    @pl.when(c == 0)
    def _():
        acc_ref[...] = jnp.zeros_like(acc_ref)

    cache = {}
    n_tiles = EXPERT_CHUNK // PACK
    for k in range(MM1_SLICES):
        rows = slice(k * EXPERT_CHUNK // MM1_SLICES, (k + 1) * EXPERT_CHUNK // MM1_SLICES)
        pre_ref[rows, :] = _dot_nt(u_ref[rows, :], xn_ref[...])
        for tile in range(k * n_tiles // MM1_SLICES, (k + 1) * n_tiles // MM1_SLICES):
            _gate_tile(pre_ref, p0_ref, n0_ref, p1_ref, r1_ref, h_ref, tile, cache)
    acc_ref[...] += jnp.dot(vt_ref[...], h_ref[...], preferred_element_type=F32)

    @pl.when(c == pl.num_programs(1) - 1)
    def _():
        y = x_ref[...] + acc_ref[...].T
        if with_norm:
            ms = jnp.mean(y * y, axis=-1, keepdims=True)
            y = y * lax.rsqrt(ms + EPS) * g_ref[...]
        o_ref[...] = y


def _experts(xn, x, u, vt, p0, n0, p1, r1, out_gain=None):
    t, d = x.shape
    tb, ec = EXPERT_TOKENS, EXPERT_CHUNK
    small = pl.BlockSpec((PEER_HEADS, ROWS_PER_CHUNK, tb), lambda i, c: (0, c, i))
    full = pl.BlockSpec((PEER_HEADS, PEER_N_KEYS, tb), lambda i, c: (0, 0, i))
    with_norm = out_gain is not None
    gain = ([pl.BlockSpec((1, d), lambda i, c: (0, 0))], [out_gain.reshape(1, d)]) if with_norm else ([], [])
    return pl.pallas_call(
        functools.partial(_expert_kernel, with_norm=with_norm), grid=(t // tb, PEER_EXPERTS // ec),
        in_specs=[pl.BlockSpec((tb, d), lambda i, c: (i, 0)),
                  pl.BlockSpec((tb, d), lambda i, c: (i, 0)),
                  pl.BlockSpec((ec, d), lambda i, c: (c, 0)),
                  pl.BlockSpec((None, d, ec), lambda i, c: (c, 0, 0)),
                  small, small, full, full] + gain[0],
        out_specs=pl.BlockSpec((tb, d), lambda i, c: (i, 0)),
        out_shape=jax.ShapeDtypeStruct((t, d), F32),
        scratch_shapes=[pltpu.VMEM((d, tb), F32), pltpu.VMEM((ec, tb), F32),
                        pltpu.VMEM((ec, tb), BF16)],
        compiler_params=_cparams("parallel", "arbitrary"), name="peer_experts",
    )(xn, x, u, vt, p0, n0, p1, r1, *gain[1])


def _peer(x, g, wq, keys, u, v, out_gain=None):
    xq, xn = _proj(x, g, wq.astype(MXU_DTYPE), k_cols=x.shape[1], tm=512, tn=wq.shape[1],
                   with_xn=True, out_dtype=MXU_DTYPE)
    keys2 = keys.reshape(2 * PEER_HEADS, PEER_N_KEYS, PEER_HALF).astype(MXU_DTYPE)
    p0, n0, p1, r1 = _router(xq, keys2)
    nc = PEER_EXPERTS // EXPERT_CHUNK
    vt = v.astype(MXU_DTYPE).reshape(nc, EXPERT_CHUNK, -1).transpose(0, 2, 1)
    return _experts(xn, x, u.astype(MXU_DTYPE), vt, p0, n0, p1, r1, out_gain)


def kernel(x, positions, norm_mix0, w_in0, sinks0, conv_w0, w_out0, norm_ffn0, peer_wq0, peer_keys0,
           peer_u0, peer_v0, norm_mix1, w_in1, g_qa1, w_uq1, w_out1, norm_ffn1, peer_wq1,
           peer_keys1, peer_u1, peer_v1, norm_final):
    b, s, d = x.shape
    t = b * s
    xf = x.reshape(t, d)
    pos = positions.reshape(t)
    tab_a = _rope_table(pos, ((0, 64), (64, 64)), 8)
    tab_b = _rope_table(pos, ((0, 128),), 16)
    tab_c = _rope_table(pos, ((0, 64),), 8)

    n0 = w_in0.shape[1]
    n_rope = (A_Q_COLS + A_KV_COLS) // LANES
    cfg0 = (0,) * n_rope + (-1,) * (n0 // LANES - n_rope)
    p0 = _proj(xf, norm_mix0, w_in0.astype(MXU_DTYPE), k_cols=d, tm=256, tn=n0,
               tabs=(tab_a,), halves=(8,), groups=(((0, 1), cfg0),))
    a_out = _swa(p0, sinks0, b, s)
    b_out = _conv(p0, conv_w0, s)
    w_o = w_out0.astype(MXU_DTYPE)
    x1 = _mm_res([a_out, b_out], [w_o[:A_Q_COLS], w_o[A_Q_COLS:]], xf, tm=512, tn=d)
    x2 = _peer(x1, norm_ffn0, peer_wq0, peer_keys0, peer_u0, peer_v0)

    w1 = jnp.pad(w_in1, ((0, 0), (0, IN1_COLS_PADDED - w_in1.shape[1]))).astype(MXU_DTYPE)
    k_chunk = C_Q_RANK // LANES
    cfg1 = tuple({k_chunk: 0, k_chunk + 2: 1}.get(c, -1) for c in range(IN1_COLS_PADDED // LANES))
    p1 = _proj(x2, norm_mix1, w1, k_cols=d, tm=512, tn=IN1_COLS_PADDED,
               tabs=(tab_b, tab_c), halves=(16, 8), groups=(((0, 1), cfg1),))
    n_q = C_HEADS * C_HEAD_DIM // LANES
    n_iq = IDX_HEADS * IDX_HEAD_DIM // LANES
    cfg_u = (0,) * n_q + (1,) * n_iq
    u = _proj(p1, g_qa1, w_uq1.astype(MXU_DTYPE), k_cols=C_Q_RANK, tm=256,
              tn=(n_q + n_iq) * LANES, tabs=(tab_b, tab_a), halves=(16, 8),
              groups=(((0, 1), cfg_u),), scales=(C_HEAD_DIM ** -0.5,) * n_q + (1.0,) * n_iq,
              out_dtype=MXU_DTYPE)
    o = _dsa(u, p1, b, s)
    x3 = _mm_res([o], [w_out1.astype(MXU_DTYPE)], x2, tm=512, tn=d)
    out = _peer(x3, norm_ffn1, peer_wq1, peer_keys1, peer_u1, peer_v1, out_gain=norm_final)
    return out.reshape(b, s, d)
```

```python
import functools

import numpy as np
import jax
import jax.numpy as jnp
from jax import lax
from jax.experimental import pallas as pl
from jax.experimental.pallas import tpu as pltpu

F32 = jnp.float32
BF16 = jnp.bfloat16
MXU_DTYPE = jnp.bfloat16

LANES = 128
EPS = 1e-5
ROPE_THETA = 500000.0
BLOCK = 128
NEG_INF = -1e30
INT_MIN = -(2 ** 31)

SWA_WINDOW = 128
A_Q_HEADS, A_KV_HEADS, A_HEAD_DIM = 16, 4, 64
A_Q_COLS = A_Q_HEADS * A_HEAD_DIM
A_KV_COLS = A_KV_HEADS * A_HEAD_DIM
B_WIDTH = 1024
C_Q_RANK, C_HEADS, C_HEAD_DIM = 512, 16, 128
IDX_HEADS, IDX_HEAD_DIM, INDEX_TOPK = 16, 64, 256
IN1_COLS_PADDED = 896
PEER_HEADS, PEER_N_KEYS, PEER_TOPK, PEER_HALF = 8, 128, 16, 128
PEER_EXPERTS = PEER_N_KEYS * PEER_N_KEYS

VMEM_LIMIT = 56 * 1024 * 1024


def _cparams(*sem):
    return pltpu.CompilerParams(dimension_semantics=sem, vmem_limit_bytes=VMEM_LIMIT)


def _resident(n_tiles):
    return pl.Buffered(1) if n_tiles == 1 else None


def _col_tiles(w, tn):
    k, n = w.shape
    return w.reshape(k, n // tn, tn).transpose(1, 0, 2)


def _dot(a, b):
    return jnp.dot(a.astype(MXU_DTYPE), b.astype(MXU_DTYPE), preferred_element_type=F32)


def _dot_nt(a, b):
    return lax.dot_general(a.astype(MXU_DTYPE), b.astype(MXU_DTYPE),
                           (((1,), (1,)), ((), ())), preferred_element_type=F32)


def _rope_table(pos, heads, half):
    rot = 2 * half
    inv = ROPE_THETA ** (-jnp.arange(half, dtype=F32) * 2.0 / rot)
    freq = np.zeros(LANES, np.int32)
    rotated = np.zeros(LANES, bool)
    upper = np.zeros(LANES, bool)
    for lane0, dh in heads:
        assert dh // 8 == half
        freq[lane0:lane0 + rot] = np.arange(rot) % half
        rotated[lane0:lane0 + rot] = True
        upper[lane0 + half:lane0 + rot] = True
    ang = pos.astype(F32)[:, None] * inv[None, :]
    cos, sin = jnp.cos(ang), jnp.sin(ang)
    onehot = (np.arange(half)[:, None] == freq[None, :]) & rotated[None, :]
    spread = lambda v, m: jnp.dot(v, jnp.asarray(onehot & m[None, :], F32),
                                  precision=lax.Precision.HIGHEST)
    c = spread(cos, rotated) + jnp.asarray(~rotated, F32)
    return jnp.stack([c, spread(sin, upper), spread(-sin, rotated & ~upper)])


def _rope_chunk(xc, tab_ref, half):
    return (xc * tab_ref[0] + pltpu.roll(xc, half, 1) * tab_ref[1]
            + pltpu.roll(xc, LANES - half, 1) * tab_ref[2])


def _proj_kernel(*refs, n_tabs, halves, groups, scales, tn, with_xn):
    x_ref, g_ref, w_ref = refs[:3]
    tab_refs = refs[3:3 + n_tabs]
    o_ref = refs[3 + n_tabs]
    xn_out = refs[4 + n_tabs] if with_xn else None
    xn_ref = refs[-1]
    j = pl.program_id(1)

    @pl.when(j == 0)
    def _():
        x = x_ref[...]
        ms = jnp.mean(x * x, axis=-1, keepdims=True)
        xn = (x * lax.rsqrt(ms + EPS) * g_ref[...]).astype(MXU_DTYPE)
        xn_ref[...] = xn
        if with_xn:
            xn_out[...] = xn

    acc = jnp.dot(xn_ref[...], w_ref[...], preferred_element_type=F32)

    for (j_lo, j_hi), cfg in groups:
        @pl.when((j >= j_lo) & (j < j_hi))
        def _(cfg=cfg):
            for c in range(tn // LANES):
                xc = acc[:, c * LANES:(c + 1) * LANES]
                if cfg[c] >= 0:
                    xc = _rope_chunk(xc, tab_refs[cfg[c]], halves[cfg[c]])
                if scales[c] != 1.0:
                    xc = xc * scales[c]
                o_ref[:, c * LANES:(c + 1) * LANES] = xc.astype(o_ref.dtype)


def _proj(x, g, w, *, k_cols, tm, tn, tabs=(), halves=(), groups=None, scales=None, with_xn=False,
          out_dtype=F32):
    t = x.shape[0]
    n = w.shape[1]
    nj = n // tn
    if groups is None:
        groups = (((0, nj), (-1,) * (tn // LANES)),)
    if scales is None:
        scales = (1.0,) * (tn // LANES)
    kern = functools.partial(_proj_kernel, n_tabs=len(tabs), halves=tuple(halves),
                             groups=tuple(groups), scales=tuple(scales), tn=tn, with_xn=with_xn)
    in_specs = [pl.BlockSpec((tm, k_cols), lambda i, j: (i, 0)),
                pl.BlockSpec((1, k_cols), lambda i, j: (0, 0)),
                pl.BlockSpec((None, k_cols, tn), lambda i, j: (j, 0, 0),
                             pipeline_mode=_resident(nj))]
    in_specs += [pl.BlockSpec((3, tm, LANES), lambda i, j: (0, i, 0)) for _ in tabs]
    out_shape = [jax.ShapeDtypeStruct((t, n), out_dtype)]
    out_specs = [pl.BlockSpec((tm, tn), lambda i, j: (i, j))]
    if with_xn:
        out_shape.append(jax.ShapeDtypeStruct((t, k_cols), MXU_DTYPE))
        out_specs.append(pl.BlockSpec((tm, k_cols), lambda i, j: (i, 0)))
    res = pl.pallas_call(
        kern, grid=(t // tm, nj), in_specs=in_specs, out_specs=out_specs, out_shape=out_shape,
        scratch_shapes=[pltpu.VMEM((tm, k_cols), MXU_DTYPE)],
        compiler_params=_cparams("parallel", "arbitrary"), name="norm_proj",
    )(x, g.reshape(1, k_cols), _col_tiles(w, tn), *tabs)
    return res if with_xn else res[0]


def _mm_res_kernel(*refs, n_lhs):
    a_refs = refs[:n_lhs]
    w_refs = refs[n_lhs:2 * n_lhs]
    x_ref = refs[2 * n_lhs]
    o_ref = refs[2 * n_lhs + 1]
    ab_refs = refs[2 * n_lhs + 2:]

    @pl.when(pl.program_id(1) == 0)
    def _():
        for a_ref, ab_ref in zip(a_refs, ab_refs):
            ab_ref[...] = a_ref[...].astype(MXU_DTYPE)

    acc = x_ref[...]
    for ab_ref, w_ref in zip(ab_refs, w_refs):
        acc = acc + jnp.dot(ab_ref[...], w_ref[...], preferred_element_type=F32)
    o_ref[...] = acc


def _mm_res(lhs, ws, x, *, tm, tn):
    t, d = x.shape
    n_lhs = len(lhs)
    in_specs = [pl.BlockSpec((tm, a.shape[1]), lambda i, j: (i, 0)) for a in lhs]
    in_specs += [pl.BlockSpec((None, w.shape[0], tn), lambda i, j: (j, 0, 0),
                              pipeline_mode=_resident(d // tn)) for w in ws]
    in_specs += [pl.BlockSpec((tm, tn), lambda i, j: (i, j))]
    return pl.pallas_call(
        functools.partial(_mm_res_kernel, n_lhs=n_lhs), grid=(t // tm, d // tn),
        in_specs=in_specs, out_specs=pl.BlockSpec((tm, tn), lambda i, j: (i, j)),
        out_shape=jax.ShapeDtypeStruct((t, d), F32),
        scratch_shapes=[pltpu.VMEM((tm, a.shape[1]), MXU_DTYPE) for a in lhs],
        compiler_params=_cparams("parallel", "arbitrary"), name="out_proj",
    )(*lhs, *[_col_tiles(w, tn) for w in ws], x)


def _swa_kernel(sink_ref, q_ref, kp_ref, kc_ref, vp_ref, vc_ref, o_ref):
    n = pl.program_id(1)
    g_sz = A_Q_HEADS // A_KV_HEADS
    dh = A_HEAD_DIM
    q = q_ref[...]
    k2 = jnp.concatenate([kp_ref[...], kc_ref[...]], axis=0).astype(MXU_DTYPE)
    v2 = jnp.concatenate([vp_ref[...], vc_ref[...]], axis=0).astype(MXU_DTYPE)
    qi = lax.broadcasted_iota(jnp.int32, (BLOCK, 2 * BLOCK), 0)
    kj = lax.broadcasted_iota(jnp.int32, (BLOCK, 2 * BLOCK), 1)
    diff = qi + BLOCK - kj
    valid = (diff >= 0) & (diff < SWA_WINDOW) & ((kj >= BLOCK) | (n > 0))
    scale = dh ** -0.5
    for h in range(A_KV_HEADS):
        kh = k2[:, h * dh:(h + 1) * dh]
        vh = v2[:, h * dh:(h + 1) * dh]
        for g in range(g_sz):
            hq = h * g_sz + g
            s = _dot_nt(q[:, hq * dh:(hq + 1) * dh], kh) * scale
            s = jnp.where(valid, s, NEG_INF)
            sink = sink_ref[hq]
            m = jnp.maximum(jnp.max(s, axis=-1, keepdims=True), sink)
            p = jnp.exp(s - m)
            denom = jnp.sum(p, axis=-1, keepdims=True) + jnp.exp(sink - m)
            o_ref[:, hq * dh:(hq + 1) * dh] = (_dot(p, vh) / denom).astype(o_ref.dtype)


def _swa(p0, sinks, b, s):
    nb = s // BLOCK
    kv_w = A_KV_COLS
    k_blk = A_Q_COLS // kv_w
    v_blk = k_blk + 1
    cur = lambda bi, n: bi * nb + n
    prev = lambda bi, n: bi * nb + jnp.maximum(n - 1, 0)
    return pl.pallas_call(
        _swa_kernel, grid=(b, nb),
        in_specs=[pl.BlockSpec(memory_space=pltpu.SMEM),
                  pl.BlockSpec((BLOCK, A_Q_COLS), lambda bi, n: (cur(bi, n), 0)),
                  pl.BlockSpec((BLOCK, kv_w), lambda bi, n: (prev(bi, n), k_blk)),
                  pl.BlockSpec((BLOCK, kv_w), lambda bi, n: (cur(bi, n), k_blk)),
                  pl.BlockSpec((BLOCK, kv_w), lambda bi, n: (prev(bi, n), v_blk)),
                  pl.BlockSpec((BLOCK, kv_w), lambda bi, n: (cur(bi, n), v_blk))],
        out_specs=pl.BlockSpec((BLOCK, A_Q_COLS), lambda bi, n: (cur(bi, n), 0)),
        out_shape=jax.ShapeDtypeStruct((b * s, A_Q_COLS), MXU_DTYPE),
        compiler_params=_cparams("parallel", "arbitrary"), name="swa",
    )(sinks, p0, p0, p0, p0, p0)


CONV_ROWS = 256
CONV_COLS = 512
HALO = 8


def _conv_kernel(bg_ref, cg_ref, hx_ref, cgh_ref, hxh_ref, w_ref, o_ref, *, tiles_per_seq):
    i = pl.program_id(0)
    z = cg_ref[...] * hx_ref[...]
    first = (i % tiles_per_seq) == 0
    zh = jnp.where(first, 0.0, cgh_ref[...] * hxh_ref[...])
    row = lax.broadcasted_iota(jnp.int32, z.shape, 0)
    z1 = pltpu.roll(z, 1, 0)
    z1 = jnp.where(row == 0, zh[HALO - 1:HALO], z1)
    z2 = pltpu.roll(z, 2, 0)
    z2 = jnp.where(row == 0, zh[HALO - 2:HALO - 1], jnp.where(row == 1, zh[HALO - 1:HALO], z2))
    w = w_ref[...]
    o_ref[...] = (bg_ref[...] * (w[0:1] * z2 + w[1:2] * z1 + w[2:3] * z)).astype(o_ref.dtype)


def _conv(p0, conv_w, s):
    t = p0.shape[0]
    col0 = A_Q_COLS + 2 * A_KV_COLS
    bg_blk, cg_blk, hx_blk = (col0 // CONV_COLS, (col0 + B_WIDTH) // CONV_COLS,
                              (col0 + 2 * B_WIDTH) // CONV_COLS)
    rh = CONV_ROWS // HALO
    halo_row = lambda i: jnp.maximum(i * rh - 1, 0)
    return pl.pallas_call(
        functools.partial(_conv_kernel, tiles_per_seq=s // CONV_ROWS),
        grid=(t // CONV_ROWS, B_WIDTH // CONV_COLS),
        in_specs=[pl.BlockSpec((CONV_ROWS, CONV_COLS), lambda i, j: (i, bg_blk + j)),
                  pl.BlockSpec((CONV_ROWS, CONV_COLS), lambda i, j: (i, cg_blk + j)),
                  pl.BlockSpec((CONV_ROWS, CONV_COLS), lambda i, j: (i, hx_blk + j)),
                  pl.BlockSpec((HALO, CONV_COLS), lambda i, j: (halo_row(i), cg_blk + j)),
                  pl.BlockSpec((HALO, CONV_COLS), lambda i, j: (halo_row(i), hx_blk + j)),
                  pl.BlockSpec((3, CONV_COLS), lambda i, j: (0, j))],
        out_specs=pl.BlockSpec((CONV_ROWS, CONV_COLS), lambda i, j: (i, j)),
        out_shape=jax.ShapeDtypeStruct((t, B_WIDTH), MXU_DTYPE),
        compiler_params=_cparams("parallel", "parallel"), name="gated_conv",
    )(p0, p0, p0, p0, p0, conv_w)


DSA_WIDTHS = 4
COUNT_CHAINS = 8


def _dsa_block(qa_ref, qb_ref, iq_ref, iwq_ref, k_ref, v_ref, ikw_ref, o_ref, key_ref, half_ref,
               n, w, k_sel):
    ik = ikw_ref[0:w, 0:IDX_HEAD_DIM].astype(MXU_DTYPE)
    iw_t = (iwq_ref[...] * ((IDX_HEADS ** -0.5) * (IDX_HEAD_DIM ** -0.5))).T
    iq = iq_ref[...]
    heads_per_chunk = LANES // IDX_HEAD_DIM
    score = jnp.zeros((w, BLOCK), F32)
    for c in range(IDX_HEADS // heads_per_chunk):
        chunk = iq[:, c * LANES:(c + 1) * LANES]
        rhs = jnp.concatenate([chunk[:, hh * IDX_HEAD_DIM:(hh + 1) * IDX_HEAD_DIM]
                               for hh in range(heads_per_chunk)], axis=0)
        lg = _dot_nt(ik, rhs)
        for hh in range(heads_per_chunk):
            row = IDX_HEAD_DIM + c * heads_per_chunk + hh
            score = score + iw_t[row:row + 1, :] * jnp.maximum(lg[:, hh * BLOCK:(hh + 1) * BLOCK], 0.0)
    score = score + 0.0
    kpos = lax.broadcasted_iota(jnp.int32, (w, BLOCK), 0)
    qpos = n * BLOCK + lax.broadcasted_iota(jnp.int32, (w, BLOCK), 1)
    bits = pltpu.bitcast(score, jnp.int32)
    key = jnp.where(bits < 0, bits ^ jnp.int32(0x7FFFFFFF), bits)
    key = jnp.where(kpos <= qpos, key, INT_MIN)
    key_ref[0:w, :] = key
    half_ref[0:w, :] = jnp.right_shift(key, 16).astype(jnp.int16)

    def search16():
        def count_ge(t):
            t16 = t.astype(jnp.int16)
            accs = [None] * COUNT_CHAINS
            for c in range(w // PACK):
                one = jnp.where(half_ref[c * PACK:(c + 1) * PACK, :] >= t16,
                                jnp.int16(1), jnp.int16(0))
                a = c % COUNT_CHAINS
                accs[a] = one if accs[a] is None else accs[a] + one
            acc = functools.reduce(lambda x, y: x + y, [a for a in accs if a is not None])
            return jnp.sum(acc.astype(F32), axis=0, keepdims=True)

        kf = float(k_sel)
        t0 = jnp.where(count_ge(jnp.zeros((1, BLOCK), jnp.int32)) >= kf,
                       jnp.int32(0), jnp.int32(-2 ** 15))

        def bit_step(it, t):
            cand = t + jnp.left_shift(jnp.int32(1), jnp.int32(14) - it)
            return jnp.where(count_ge(cand) >= kf, cand, t)

        return lax.fori_loop(0, 15, bit_step, t0)

    t_hi = search16()
    hi = half_ref[0:w, :].astype(jnp.int32)
    lo = jnp.bitwise_and(key_ref[0:w, :], 0xFFFF) - 2 ** 15
    half_ref[0:w, :] = jnp.where(hi > t_hi, 2 ** 15 - 1,
                                 jnp.where(hi < t_hi, -2 ** 15, lo)).astype(jnp.int16)
    t_lo = search16()
    thr = jnp.maximum(t_hi * 2 ** 16 + (t_lo + 2 ** 15), INT_MIN + 1)
    bias = jnp.where(key_ref[0:w, :] >= thr, 0.0, NEG_INF).T

    kk = k_ref[0:w, :].astype(MXU_DTYPE)
    lane = lax.broadcasted_iota(jnp.int32, (w, LANES), 1)
    v_aug = jnp.concatenate([v_ref[0:w, :].astype(MXU_DTYPE),
                             jnp.where(lane == 0, 1.0, 0.0).astype(MXU_DTYPE)], axis=1)
    for h in range(C_HEADS):
        q_r = qa_ref if h < C_HEADS // 2 else qb_ref
        hh = h % (C_HEADS // 2)
        s = _dot_nt(q_r[:, hh * C_HEAD_DIM:(hh + 1) * C_HEAD_DIM], kk) + bias
        p = jnp.exp(s - jnp.max(s, axis=-1, keepdims=True))
        pv = _dot(p, v_aug)
        o_ref[:, h * C_HEAD_DIM:(h + 1) * C_HEAD_DIM] = (
            pv[:, :C_HEAD_DIM] / pv[:, C_HEAD_DIM:C_HEAD_DIM + 1]).astype(o_ref.dtype)


def _dsa_kernel(qa_ref, qb_ref, iq_ref, iwq_ref, k_ref, v_ref, ikw_ref, o_ref, key_ref, half_ref,
                *, k_sel):
    n = pl.program_id(1)
    chunk = k_ref.shape[0] // DSA_WIDTHS
    n_chunks = (n * BLOCK + BLOCK - 1) // chunk + 1
    for wi in range(1, DSA_WIDTHS + 1):
        if (wi * chunk) % BLOCK:
            continue

        @pl.when(n_chunks == wi)
        def _(wi=wi):
            _dsa_block(qa_ref, qb_ref, iq_ref, iwq_ref, k_ref, v_ref, ikw_ref, o_ref, key_ref,
                       half_ref, n, wi * chunk, k_sel)


def _dsa(u, p1, b, s):
    nb = s // BLOCK
    k_sel = min(INDEX_TOPK, s // 4)
    half_q = C_HEADS * C_HEAD_DIM // 2
    k_blk = C_Q_RANK // LANES
    row = lambda bi, n: bi * nb + n
    return pl.pallas_call(
        functools.partial(_dsa_kernel, k_sel=k_sel), grid=(b, nb),
        in_specs=[pl.BlockSpec((BLOCK, half_q), lambda bi, n: (row(bi, n), 0)),
                  pl.BlockSpec((BLOCK, half_q), lambda bi, n: (row(bi, n), 1)),
                  pl.BlockSpec((BLOCK, half_q), lambda bi, n: (row(bi, n), 2)),
                  pl.BlockSpec((BLOCK, LANES), lambda bi, n: (row(bi, n), k_blk + 2)),
                  pl.BlockSpec((s, LANES), lambda bi, n: (bi, k_blk)),
                  pl.BlockSpec((s, LANES), lambda bi, n: (bi, k_blk + 1)),
                  pl.BlockSpec((s, LANES), lambda bi, n: (bi, k_blk + 2))],
        out_specs=pl.BlockSpec((BLOCK, C_HEADS * C_HEAD_DIM), lambda bi, n: (row(bi, n), 0)),
        out_shape=jax.ShapeDtypeStruct((b * s, C_HEADS * C_HEAD_DIM), MXU_DTYPE),
        scratch_shapes=[pltpu.VMEM((s, BLOCK), jnp.int32), pltpu.VMEM((s, BLOCK), jnp.int16)],
        compiler_params=_cparams("parallel", "arbitrary"), name="dsa",
    )(u, u, u, p1, p1, p1, p1)


ROUTER_TOKENS = 512
ROUTER_UNROLL = 2
N_FULL_K0 = 8
SUB = 8


def _top_rows(work, k, exact):
    rows, cols = work.shape
    vrow = lax.broadcasted_iota(jnp.int32, (k, cols), 0)
    vals = jnp.zeros((k, cols), F32)
    if exact:
        row = lax.broadcasted_iota(jnp.int32, (rows, cols), 0).astype(F32)
        rank = jnp.full((rows, cols), float(k), F32)
    for r in range(k):
        m = jnp.max(work, axis=0, keepdims=True)
        hit = work == m
        if exact:
            first = jnp.min(jnp.where(hit, row, float(rows)), axis=0, keepdims=True)
            hit = row == first
            rank = jnp.where(hit, float(r), rank)
            work = jnp.where(hit, -jnp.inf, work)
        else:
            work = jnp.where(hit, -(2.0 ** 127) * (1.0 + r / 16.0), work)
        vals = jnp.where(vrow == r, m, vals)
    if not exact:
        assert k == 16
        rank = jnp.where(work <= -(2.0 ** 127), work * -(2.0 ** -123) - 16.0, float(k))
    return rank, vals


def _count_ranked(rank, k):
    return jnp.sum(jnp.where(rank < float(k), 1.0, 0.0), axis=0, keepdims=True)


def _pair_counts_exact(v0, v1, k):
    cand = jnp.concatenate([v0[k0:k0 + 1] + v1 for k0 in range(N_FULL_K0)]
                           + [v0[N_FULL_K0:] + v1[0:1]], axis=0)
    rank_c, _ = _top_rows(cand, k, True)
    sel = rank_c < float(k)
    cnt = jnp.where(sel, 1.0, 0.0)
    n_rank = jnp.concatenate(
        [jnp.sum(cnt[k0 * k:(k0 + 1) * k], axis=0, keepdims=True) for k0 in range(N_FULL_K0)]
        + [cnt[N_FULL_K0 * k:]], axis=0)
    z = jnp.sum(jnp.where(sel, jnp.exp(cand - cand[0:1]), 0.0), axis=0, keepdims=True)
    return n_rank, z, jnp.sum(cnt, axis=0, keepdims=True)


def _pair_counts_fast(v0, v1, k):
    cols = v0.shape[1]
    row = lax.broadcasted_iota(jnp.int32, (SUB, cols), 0)
    tiles = []
    for k1 in range(SUB):
        n_valid = min(k // (k1 + 1), SUB)
        tiles.append(jnp.where(row < n_valid, v0[0:SUB] + v1[k1:k1 + 1], -jnp.inf))
    tiles.append(v0[0:1] + v1[SUB:])
    tiles.append(v0[SUB:] + v1[0:1])
    cand = jnp.concatenate(tiles, axis=0)
    rank_c, _ = _top_rows(cand, k, False)
    sel = rank_c < float(k)
    cnt = jnp.where(sel, 1.0, 0.0)
    low = cnt[0:SUB]
    for k1 in range(1, SUB):
        low = low + cnt[k1 * SUB:(k1 + 1) * SUB]
    tail = jnp.sum(cnt[SUB * SUB:SUB * SUB + SUB], axis=0, keepdims=True)
    low = low + jnp.where(row == 0, tail, 0.0)
    n_rank = jnp.concatenate([low, cnt[SUB * SUB + SUB:]], axis=0)
    z = jnp.sum(jnp.where(sel, jnp.exp(cand - (v0[0:1] + v1[0:1])), 0.0), axis=0, keepdims=True)
    return n_rank, z, jnp.sum(cnt, axis=0, keepdims=True)


def _router_kernel(q_ref, keys_ref, p0_ref, n0_ref, p1_ref, r1_ref):
    k = PEER_TOPK

    def route(h, exact):
        col0 = pl.multiple_of(h * (2 * PEER_HALF), LANES)
        col1 = pl.multiple_of(col0 + PEER_HALF, LANES)
        s0 = _dot_nt(keys_ref[2 * h], q_ref[:, pl.ds(col0, PEER_HALF)])
        s1 = _dot_nt(keys_ref[2 * h + 1], q_ref[:, pl.ds(col1, PEER_HALF)])
        rank0, v0 = _top_rows(s0, k, exact)
        rank1, v1 = _top_rows(s1, k, exact)
        n_rank, z, n_sel = (_pair_counts_exact if exact else _pair_counts_fast)(v0, v1, k)
        n0 = jnp.zeros_like(s0)
        for r in range(k):
            n0 = jnp.where(rank0 == float(r), n_rank[r:r + 1], n0)
        p0_ref[h] = jnp.where(rank0 < float(k), jnp.exp(s0 - v0[0:1]), 0.0) / z
        n0_ref[h] = n0
        p1_ref[h] = jnp.where(rank1 < float(k), jnp.exp(s1 - v1[0:1]), 0.0).astype(p1_ref.dtype)
        r1_ref[h] = rank1.astype(r1_ref.dtype)
        return _count_ranked(rank0, k) + _count_ranked(rank1, k) + n_sel

    def fast_heads(i, worst):
        for j in range(ROUTER_UNROLL):
            worst = jnp.maximum(worst, route(i * ROUTER_UNROLL + j, False))
        return worst

    worst = lax.fori_loop(0, PEER_HEADS // ROUTER_UNROLL, fast_heads,
                          jnp.zeros((1, q_ref.shape[0]), F32))

    @pl.when(jnp.max(worst) > 3.0 * k)
    def _():
        def exact_head(h, carry):
            route(h, True)
            return carry
        lax.fori_loop(0, PEER_HEADS, exact_head, 0)


def _router(q, keys):
    t = q.shape[0]
    tb = ROUTER_TOKENS
    blk = pl.BlockSpec((PEER_HEADS, PEER_N_KEYS, tb), lambda i: (0, 0, i))
    f32s = jax.ShapeDtypeStruct((PEER_HEADS, PEER_N_KEYS, t), F32)
    bf16s = jax.ShapeDtypeStruct((PEER_HEADS, PEER_N_KEYS, t), BF16)
    return pl.pallas_call(
        _router_kernel, grid=(t // tb,),
        in_specs=[pl.BlockSpec((tb, q.shape[1]), lambda i: (i, 0)),
                  pl.BlockSpec(keys.shape, lambda i: (0, 0, 0))],
        out_specs=[blk, blk, blk, blk], out_shape=[f32s, f32s, bf16s, bf16s],
        compiler_params=_cparams("parallel"), name="peer_router",
    )(q, keys)


EXPERT_TOKENS = 512
EXPERT_CHUNK = 1024
ROWS_PER_CHUNK = EXPERT_CHUNK // PEER_N_KEYS


MM1_SLICES = 2
PACK = 16


def _gate_tile(pre_ref, p0_ref, n0_ref, p1_ref, r1_ref, h_ref, tile, cache):
    tb = pre_ref.shape[1]
    ii, rt = divmod(tile, PEER_N_KEYS // PACK)
    if ii not in cache:
        cache.clear()
        cache[ii] = (
            [jnp.broadcast_to(p0_ref[h, ii:ii + 1, :], (PACK, tb)).astype(BF16)
             for h in range(PEER_HEADS)],
            [jnp.broadcast_to(n0_ref[h, ii:ii + 1, :], (PACK, tb)).astype(BF16)
             for h in range(PEER_HEADS)])
    gates, cnts = cache[ii]
    js = slice(rt * PACK, (rt + 1) * PACK)
    w = None
    for h in range(PEER_HEADS):
        term = jnp.where(r1_ref[h, js, :] < cnts[h], p1_ref[h, js, :],
                         jnp.zeros((), BF16)) * gates[h]
        w = term if w is None else w + term
    rows = slice(tile * PACK, (tile + 1) * PACK)
    pre = pre_ref[rows, :]
    act = 0.5 * pre * (1.0 + lax.erf(pre * np.sqrt(0.5).astype(np.float32)))
    h_ref[rows, :] = act.astype(BF16) * w


def _expert_kernel(*refs, with_norm):
    xn_ref, x_ref, u_ref, vt_ref, p0_ref, n0_ref, p1_ref, r1_ref = refs[:8]
    g_ref = refs[8] if with_norm else None
    o_ref, acc_ref, pre_ref, h_ref = refs[-4:]
    c = pl.program_id(1)

    @pl.when(c == 0)
    def _():
        acc_ref[...] = jnp.zeros_like(acc_ref)

    cache = {}
    n_tiles = EXPERT_CHUNK // PACK
    for k in range(MM1_SLICES):
        rows = slice(k * EXPERT_CHUNK // MM1_SLICES, (k + 1) * EXPERT_CHUNK // MM1_SLICES)
        pre_ref[rows, :] = _dot_nt(u_ref[rows, :], xn_ref[...])
        for tile in range(k * n_tiles // MM1_SLICES, (k + 1) * n_tiles // MM1_SLICES):
            _gate_tile(pre_ref, p0_ref, n0_ref, p1_ref, r1_ref, h_ref, tile, cache)
    acc_ref[...] += jnp.dot(vt_ref[...], h_ref[...], preferred_element_type=F32)

    @pl.when(c == pl.num_programs(1) - 1)
    def _():
        y = x_ref[...] + acc_ref[...].T
        if with_norm:
            ms = jnp.mean(y * y, axis=-1, keepdims=True)
            y = y * lax.rsqrt(ms + EPS) * g_ref[...]
        o_ref[...] = y


def _experts(xn, x, u, vt, p0, n0, p1, r1, out_gain=None):
    t, d = x.shape
    tb, ec = EXPERT_TOKENS, EXPERT_CHUNK
    small = pl.BlockSpec((PEER_HEADS, ROWS_PER_CHUNK, tb), lambda i, c: (0, c, i))
    full = pl.BlockSpec((PEER_HEADS, PEER_N_KEYS, tb), lambda i, c: (0, 0, i))
    with_norm = out_gain is not None
    gain = ([pl.BlockSpec((1, d), lambda i, c: (0, 0))], [out_gain.reshape(1, d)]) if with_norm else ([], [])
    return pl.pallas_call(
        functools.partial(_expert_kernel, with_norm=with_norm), grid=(t // tb, PEER_EXPERTS // ec),
        in_specs=[pl.BlockSpec((tb, d), lambda i, c: (i, 0)),
                  pl.BlockSpec((tb, d), lambda i, c: (i, 0)),
                  pl.BlockSpec((ec, d), lambda i, c: (c, 0)),
                  pl.BlockSpec((None, d, ec), lambda i, c: (c, 0, 0)),
                  small, small, full, full] + gain[0],
        out_specs=pl.BlockSpec((tb, d), lambda i, c: (i, 0)),
        out_shape=jax.ShapeDtypeStruct((t, d), F32),
        scratch_shapes=[pltpu.VMEM((d, tb), F32), pltpu.VMEM((ec, tb), F32),
                        pltpu.VMEM((ec, tb), BF16)],
        compiler_params=_cparams("parallel", "arbitrary"), name="peer_experts",
    )(xn, x, u, vt, p0, n0, p1, r1, *gain[1])


def _peer(x, g, wq, keys, u, v, out_gain=None):
    xq, xn = _proj(x, g, wq.astype(MXU_DTYPE), k_cols=x.shape[1], tm=512, tn=wq.shape[1],
                   with_xn=True, out_dtype=MXU_DTYPE)
    keys2 = keys.reshape(2 * PEER_HEADS, PEER_N_KEYS, PEER_HALF).astype(MXU_DTYPE)
    p0, n0, p1, r1 = _router(xq, keys2)
    nc = PEER_EXPERTS // EXPERT_CHUNK
    vt = v.astype(MXU_DTYPE).reshape(nc, EXPERT_CHUNK, -1).transpose(0, 2, 1)
    return _experts(xn, x, u.astype(MXU_DTYPE), vt, p0, n0, p1, r1, out_gain)


def kernel(x, positions, norm_mix0, w_in0, sinks0, conv_w0, w_out0, norm_ffn0, peer_wq0, peer_keys0,
           peer_u0, peer_v0, norm_mix1, w_in1, g_qa1, w_uq1, w_out1, norm_ffn1, peer_wq1,
           peer_keys1, peer_u1, peer_v1, norm_final):
    b, s, d = x.shape
    t = b * s
    xf = x.reshape(t, d)
    pos = positions.reshape(t)
    tab_a = _rope_table(pos, ((0, 64), (64, 64)), 8)
    tab_b = _rope_table(pos, ((0, 128),), 16)
    tab_c = _rope_table(pos, ((0, 64),), 8)

    n0 = w_in0.shape[1]
    n_rope = (A_Q_COLS + A_KV_COLS) // LANES
    cfg0 = (0,) * n_rope + (-1,) * (n0 // LANES - n_rope)
    p0 = _proj(xf, norm_mix0, w_in0.astype(MXU_DTYPE), k_cols=d, tm=256, tn=n0,
               tabs=(tab_a,), halves=(8,), groups=(((0, 1), cfg0),))
    a_out = _swa(p0, sinks0, b, s)
    b_out = _conv(p0, conv_w0, s)
    w_o = w_out0.astype(MXU_DTYPE)
    x1 = _mm_res([a_out, b_out], [w_o[:A_Q_COLS], w_o[A_Q_COLS:]], xf, tm=512, tn=d)
    x2 = _peer(x1, norm_ffn0, peer_wq0, peer_keys0, peer_u0, peer_v0)

    w1 = jnp.pad(w_in1, ((0, 0), (0, IN1_COLS_PADDED - w_in1.shape[1]))).astype(MXU_DTYPE)
    k_chunk = C_Q_RANK // LANES
    cfg1 = tuple({k_chunk: 0, k_chunk + 2: 1}.get(c, -1) for c in range(IN1_COLS_PADDED // LANES))
    p1 = _proj(x2, norm_mix1, w1, k_cols=d, tm=512, tn=IN1_COLS_PADDED,
               tabs=(tab_b, tab_c), halves=(16, 8), groups=(((0, 1), cfg1),))
    n_q = C_HEADS * C_HEAD_DIM // LANES
    n_iq = IDX_HEADS * IDX_HEAD_DIM // LANES
    cfg_u = (0,) * n_q + (1,) * n_iq
    u = _proj(p1, g_qa1, w_uq1.astype(MXU_DTYPE), k_cols=C_Q_RANK, tm=256,
              tn=(n_q + n_iq) * LANES, tabs=(tab_b, tab_a), halves=(16, 8),
              groups=(((0, 1), cfg_u),), scales=(C_HEAD_DIM ** -0.5,) * n_q + (1.0,) * n_iq,
              out_dtype=MXU_DTYPE)
    o = _dsa(u, p1, b, s)
    x3 = _mm_res([o], [w_out1.astype(MXU_DTYPE)], x2, tm=512, tn=d)
    out = _peer(x3, norm_ffn1, peer_wq1, peer_keys1, peer_u1, peer_v1, out_gain=norm_final)
    return out.reshape(b, s, d)
```

```python
import functools

import numpy as np
import jax
import jax.numpy as jnp
from jax import lax
from jax.experimental import pallas as pl
from jax.experimental.pallas import tpu as pltpu

F32 = jnp.float32
BF16 = jnp.bfloat16
MXU_DTYPE = jnp.bfloat16

LANES = 128
EPS = 1e-5
ROPE_THETA = 500000.0
BLOCK = 128
NEG_INF = -1e30
INT_MIN = -(2 ** 31)

SWA_WINDOW = 128
A_Q_HEADS, A_KV_HEADS, A_HEAD_DIM = 16, 4, 64
A_Q_COLS = A_Q_HEADS * A_HEAD_DIM
A_KV_COLS = A_KV_HEADS * A_HEAD_DIM
B_WIDTH = 1024
C_Q_RANK, C_HEADS, C_HEAD_DIM = 512, 16, 128
IDX_HEADS, IDX_HEAD_DIM, INDEX_TOPK = 16, 64, 256
IN1_COLS_PADDED = 896
PEER_HEADS, PEER_N_KEYS, PEER_TOPK, PEER_HALF = 8, 128, 16, 128
PEER_EXPERTS = PEER_N_KEYS * PEER_N_KEYS

VMEM_LIMIT = 56 * 1024 * 1024


def _cparams(*sem):
    return pltpu.CompilerParams(dimension_semantics=sem, vmem_limit_bytes=VMEM_LIMIT)


def _resident(n_tiles):
    return pl.Buffered(1) if n_tiles == 1 else None


def _col_tiles(w, tn):
    k, n = w.shape
    return w.reshape(k, n // tn, tn).transpose(1, 0, 2)


def _dot(a, b):
    return jnp.dot(a.astype(MXU_DTYPE), b.astype(MXU_DTYPE), preferred_element_type=F32)


def _dot_nt(a, b):
    return lax.dot_general(a.astype(MXU_DTYPE), b.astype(MXU_DTYPE),
                           (((1,), (1,)), ((), ())), preferred_element_type=F32)


def _rope_table(pos, heads, half):
    rot = 2 * half
    inv = ROPE_THETA ** (-jnp.arange(half, dtype=F32) * 2.0 / rot)
    freq = np.zeros(LANES, np.int32)
    rotated = np.zeros(LANES, bool)
    upper = np.zeros(LANES, bool)
    for lane0, dh in heads:
        assert dh // 8 == half
        freq[lane0:lane0 + rot] = np.arange(rot) % half
        rotated[lane0:lane0 + rot] = True
        upper[lane0 + half:lane0 + rot] = True
    ang = pos.astype(F32)[:, None] * inv[None, :]
    cos, sin = jnp.cos(ang), jnp.sin(ang)
    onehot = (np.arange(half)[:, None] == freq[None, :]) & rotated[None, :]
    spread = lambda v, m: jnp.dot(v, jnp.asarray(onehot & m[None, :], F32),
                                  precision=lax.Precision.HIGHEST)
    c = spread(cos, rotated) + jnp.asarray(~rotated, F32)
    return jnp.stack([c, spread(sin, upper), spread(-sin, rotated & ~upper)])


def _rope_chunk(xc, tab_ref, half):
    return (xc * tab_ref[0] + pltpu.roll(xc, half, 1) * tab_ref[1]
            + pltpu.roll(xc, LANES - half, 1) * tab_ref[2])


def _proj_kernel(*refs, n_tabs, halves, groups, scales, tn, with_xn):
    x_ref, g_ref, w_ref = refs[:3]
    tab_refs = refs[3:3 + n_tabs]
    o_ref = refs[3 + n_tabs]
    xn_out = refs[4 + n_tabs] if with_xn else None
    xn_ref = refs[-1]
    j = pl.program_id(1)

    @pl.when(j == 0)
    def _():
        x = x_ref[...]
        ms = jnp.mean(x * x, axis=-1, keepdims=True)
        xn = (x * lax.rsqrt(ms + EPS) * g_ref[...]).astype(MXU_DTYPE)
        xn_ref[...] = xn
        if with_xn:
            xn_out[...] = xn

    acc = jnp.dot(xn_ref[...], w_ref[...], preferred_element_type=F32)

    for (j_lo, j_hi), cfg in groups:
        @pl.when((j >= j_lo) & (j < j_hi))
        def _(cfg=cfg):
            for c in range(tn // LANES):
                xc = acc[:, c * LANES:(c + 1) * LANES]
                if cfg[c] >= 0:
                    xc = _rope_chunk(xc, tab_refs[cfg[c]], halves[cfg[c]])
                if scales[c] != 1.0:
                    xc = xc * scales[c]
                o_ref[:, c * LANES:(c + 1) * LANES] = xc.astype(o_ref.dtype)


def _proj(x, g, w, *, k_cols, tm, tn, tabs=(), halves=(), groups=None, scales=None, with_xn=False,
          out_dtype=F32):
    t = x.shape[0]
    n = w.shape[1]
    nj = n // tn
    if groups is None:
        groups = (((0, nj), (-1,) * (tn // LANES)),)
    if scales is None:
        scales = (1.0,) * (tn // LANES)
    kern = functools.partial(_proj_kernel, n_tabs=len(tabs), halves=tuple(halves),
                             groups=tuple(groups), scales=tuple(scales), tn=tn, with_xn=with_xn)
    in_specs = [pl.BlockSpec((tm, k_cols), lambda i, j: (i, 0)),
                pl.BlockSpec((1, k_cols), lambda i, j: (0, 0)),
                pl.BlockSpec((None, k_cols, tn), lambda i, j: (j, 0, 0),
                             pipeline_mode=_resident(nj))]
    in_specs += [pl.BlockSpec((3, tm, LANES), lambda i, j: (0, i, 0)) for _ in tabs]
    out_shape = [jax.ShapeDtypeStruct((t, n), out_dtype)]
    out_specs = [pl.BlockSpec((tm, tn), lambda i, j: (i, j))]
    if with_xn:
        out_shape.append(jax.ShapeDtypeStruct((t, k_cols), MXU_DTYPE))
        out_specs.append(pl.BlockSpec((tm, k_cols), lambda i, j: (i, 0)))
    res = pl.pallas_call(
        kern, grid=(t // tm, nj), in_specs=in_specs, out_specs=out_specs, out_shape=out_shape,
        scratch_shapes=[pltpu.VMEM((tm, k_cols), MXU_DTYPE)],
        compiler_params=_cparams("parallel", "arbitrary"), name="norm_proj",
    )(x, g.reshape(1, k_cols), _col_tiles(w, tn), *tabs)
    return res if with_xn else res[0]


def _mm_res_kernel(*refs, n_lhs):
    a_refs = refs[:n_lhs]
    w_refs = refs[n_lhs:2 * n_lhs]
    x_ref = refs[2 * n_lhs]
    o_ref = refs[2 * n_lhs + 1]
    ab_refs = refs[2 * n_lhs + 2:]

    @pl.when(pl.program_id(1) == 0)
    def _():
        for a_ref, ab_ref in zip(a_refs, ab_refs):
            ab_ref[...] = a_ref[...].astype(MXU_DTYPE)

    acc = x_ref[...]
    for ab_ref, w_ref in zip(ab_refs, w_refs):
        acc = acc + jnp.dot(ab_ref[...], w_ref[...], preferred_element_type=F32)
    o_ref[...] = acc


def _mm_res(lhs, ws, x, *, tm, tn):
    t, d = x.shape
    n_lhs = len(lhs)
    in_specs = [pl.BlockSpec((tm, a.shape[1]), lambda i, j: (i, 0)) for a in lhs]
    in_specs += [pl.BlockSpec((None, w.shape[0], tn), lambda i, j: (j, 0, 0),
                              pipeline_mode=_resident(d // tn)) for w in ws]
    in_specs += [pl.BlockSpec((tm, tn), lambda i, j: (i, j))]
    return pl.pallas_call(
        functools.partial(_mm_res_kernel, n_lhs=n_lhs), grid=(t // tm, d // tn),
        in_specs=in_specs, out_specs=pl.BlockSpec((tm, tn), lambda i, j: (i, j)),
        out_shape=jax.ShapeDtypeStruct((t, d), F32),
        scratch_shapes=[pltpu.VMEM((tm, a.shape[1]), MXU_DTYPE) for a in lhs],
        compiler_params=_cparams("parallel", "arbitrary"), name="out_proj",
    )(*lhs, *[_col_tiles(w, tn) for w in ws], x)


def _swa_kernel(sink_ref, q_ref, kp_ref, kc_ref, vp_ref, vc_ref, o_ref):
    n = pl.program_id(1)
    g_sz = A_Q_HEADS // A_KV_HEADS
    dh = A_HEAD_DIM
    q = q_ref[...]
    k2 = jnp.concatenate([kp_ref[...], kc_ref[...]], axis=0).astype(MXU_DTYPE)
    v2 = jnp.concatenate([vp_ref[...], vc_ref[...]], axis=0).astype(MXU_DTYPE)
    qi = lax.broadcasted_iota(jnp.int32, (BLOCK, 2 * BLOCK), 0)
    kj = lax.broadcasted_iota(jnp.int32, (BLOCK, 2 * BLOCK), 1)
    diff = qi + BLOCK - kj
    valid = (diff >= 0) & (diff < SWA_WINDOW) & ((kj >= BLOCK) | (n > 0))
    scale = dh ** -0.5
    for h in range(A_KV_HEADS):
        kh = k2[:, h * dh:(h + 1) * dh]
        vh = v2[:, h * dh:(h + 1) * dh]
        for g in range(g_sz):
            hq = h * g_sz + g
            s = _dot_nt(q[:, hq * dh:(hq + 1) * dh], kh) * scale
            s = jnp.where(valid, s, NEG_INF)
            sink = sink_ref[hq]
            m = jnp.maximum(jnp.max(s, axis=-1, keepdims=True), sink)
            p = jnp.exp(s - m)
            denom = jnp.sum(p, axis=-1, keepdims=True) + jnp.exp(sink - m)
            o_ref[:, hq * dh:(hq + 1) * dh] = (_dot(p, vh) / denom).astype(o_ref.dtype)


def _swa(p0, sinks, b, s):
    nb = s // BLOCK
    kv_w = A_KV_COLS
    k_blk = A_Q_COLS // kv_w
    v_blk = k_blk + 1
    cur = lambda bi, n: bi * nb + n
    prev = lambda bi, n: bi * nb + jnp.maximum(n - 1, 0)
    return pl.pallas_call(
        _swa_kernel, grid=(b, nb),
        in_specs=[pl.BlockSpec(memory_space=pltpu.SMEM),
                  pl.BlockSpec((BLOCK, A_Q_COLS), lambda bi, n: (cur(bi, n), 0)),
                  pl.BlockSpec((BLOCK, kv_w), lambda bi, n: (prev(bi, n), k_blk)),
                  pl.BlockSpec((BLOCK, kv_w), lambda bi, n: (cur(bi, n), k_blk)),
                  pl.BlockSpec((BLOCK, kv_w), lambda bi, n: (prev(bi, n), v_blk)),
                  pl.BlockSpec((BLOCK, kv_w), lambda bi, n: (cur(bi, n), v_blk))],
        out_specs=pl.BlockSpec((BLOCK, A_Q_COLS), lambda bi, n: (cur(bi, n), 0)),
        out_shape=jax.ShapeDtypeStruct((b * s, A_Q_COLS), MXU_DTYPE),
        compiler_params=_cparams("parallel", "arbitrary"), name="swa",
    )(sinks, p0, p0, p0, p0, p0)


IN0_ROWS = 256
HALO = 8


def _in0_kernel(x_ref, g_ref, w_ref, tab_ref, cw_ref, qkv_ref, b_ref, zprev_ref, *, tiles_per_seq):
    i = pl.program_id(0)

    @pl.when(i == 0)
    def _():
        zprev_ref[...] = jnp.zeros_like(zprev_ref)

    x = x_ref[...]
    ms = jnp.mean(x * x, axis=-1, keepdims=True)
    xn = (x * lax.rsqrt(ms + EPS) * g_ref[...]).astype(MXU_DTYPE)
    acc = jnp.dot(xn, w_ref[...], preferred_element_type=F32)
    qkv_cols = A_Q_COLS + 2 * A_KV_COLS
    for c in range(qkv_cols // LANES):
        xc = acc[:, c * LANES:(c + 1) * LANES]
        if c < (A_Q_COLS + A_KV_COLS) // LANES:
            xc = _rope_chunk(xc, tab_ref, A_HEAD_DIM // 8)
        qkv_ref[:, c * LANES:(c + 1) * LANES] = xc
    bg = acc[:, qkv_cols:qkv_cols + B_WIDTH]
    z = acc[:, qkv_cols + B_WIDTH:qkv_cols + 2 * B_WIDTH] * acc[:, qkv_cols + 2 * B_WIDTH:]
    zp = jnp.where((i % tiles_per_seq) == 0, 0.0, zprev_ref[...])
    row = lax.broadcasted_iota(jnp.int32, z.shape, 0)
    z1 = jnp.where(row == 0, zp[HALO - 1:HALO], pltpu.roll(z, 1, 0))
    z2 = jnp.where(row == 0, zp[HALO - 2:HALO - 1],
                   jnp.where(row == 1, zp[HALO - 1:HALO], pltpu.roll(z, 2, 0)))
    w = cw_ref[...]
    b_ref[...] = (bg * (w[0:1] * z2 + w[1:2] * z1 + w[2:3] * z)).astype(b_ref.dtype)
    zprev_ref[...] = z[z.shape[0] - HALO:, :]


def _in0(x, g, w, tab, conv_w, s):
    t, d = x.shape
    n = w.shape[1]
    qkv_cols = A_Q_COLS + 2 * A_KV_COLS
    tm = IN0_ROWS
    return pl.pallas_call(
        functools.partial(_in0_kernel, tiles_per_seq=s // tm), grid=(t // tm,),
        in_specs=[pl.BlockSpec((tm, d), lambda i: (i, 0)),
                  pl.BlockSpec((1, d), lambda i: (0, 0)),
                  pl.BlockSpec((d, n), lambda i: (0, 0), pipeline_mode=pl.Buffered(1)),
                  pl.BlockSpec((3, tm, LANES), lambda i: (0, i, 0)),
                  pl.BlockSpec((3, B_WIDTH), lambda i: (0, 0))],
        out_specs=[pl.BlockSpec((tm, qkv_cols), lambda i: (i, 0)),
                   pl.BlockSpec((tm, B_WIDTH), lambda i: (i, 0))],
        out_shape=[jax.ShapeDtypeStruct((t, qkv_cols), F32),
                   jax.ShapeDtypeStruct((t, B_WIDTH), MXU_DTYPE)],
        scratch_shapes=[pltpu.VMEM((HALO, B_WIDTH), F32)],
        compiler_params=_cparams("arbitrary"), name="in0_conv",
    )(x, g.reshape(1, d), w, tab, conv_w)


DSA_WIDTHS = 4
COUNT_CHAINS = 8


def _dsa_block(qa_ref, qb_ref, iq_ref, iwq_ref, k_ref, v_ref, ikw_ref, o_ref, key_ref, half_ref,
               n, w, k_sel):
    ik = ikw_ref[0:w, 0:IDX_HEAD_DIM].astype(MXU_DTYPE)
    iw_t = (iwq_ref[...] * ((IDX_HEADS ** -0.5) * (IDX_HEAD_DIM ** -0.5))).T
    iq = iq_ref[...]
    heads_per_chunk = LANES // IDX_HEAD_DIM
    score = jnp.zeros((w, BLOCK), F32)
    for c in range(IDX_HEADS // heads_per_chunk):
        chunk = iq[:, c * LANES:(c + 1) * LANES]
        rhs = jnp.concatenate([chunk[:, hh * IDX_HEAD_DIM:(hh + 1) * IDX_HEAD_DIM]
                               for hh in range(heads_per_chunk)], axis=0)
        lg = _dot_nt(ik, rhs)
        for hh in range(heads_per_chunk):
            row = IDX_HEAD_DIM + c * heads_per_chunk + hh
            score = score + iw_t[row:row + 1, :] * jnp.maximum(lg[:, hh * BLOCK:(hh + 1) * BLOCK], 0.0)
    score = score + 0.0
    kpos = lax.broadcasted_iota(jnp.int32, (w, BLOCK), 0)
    qpos = n * BLOCK + lax.broadcasted_iota(jnp.int32, (w, BLOCK), 1)
    bits = pltpu.bitcast(score, jnp.int32)
    key = jnp.where(bits < 0, bits ^ jnp.int32(0x7FFFFFFF), bits)
    key = jnp.where(kpos <= qpos, key, INT_MIN)
    key_ref[0:w, :] = key
    half_ref[0:w, :] = jnp.right_shift(key, 16).astype(jnp.int16)

    def search16():
        def count_ge(t):
            t16 = t.astype(jnp.int16)
            accs = [None] * COUNT_CHAINS
            for c in range(w // PACK):
                one = jnp.where(half_ref[c * PACK:(c + 1) * PACK, :] >= t16,
                                jnp.int16(1), jnp.int16(0))
                a = c % COUNT_CHAINS
                accs[a] = one if accs[a] is None else accs[a] + one
            acc = functools.reduce(lambda x, y: x + y, [a for a in accs if a is not None])
            return jnp.sum(acc.astype(F32), axis=0, keepdims=True)

        kf = float(k_sel)
        t0 = jnp.where(count_ge(jnp.zeros((1, BLOCK), jnp.int32)) >= kf,
                       jnp.int32(0), jnp.int32(-2 ** 15))

        def bit_step(it, t):
            cand = t + jnp.left_shift(jnp.int32(1), jnp.int32(14) - it)
            return jnp.where(count_ge(cand) >= kf, cand, t)

        return lax.fori_loop(0, 15, bit_step, t0)

    t_hi = search16()
    hi = half_ref[0:w, :].astype(jnp.int32)
    lo = jnp.bitwise_and(key_ref[0:w, :], 0xFFFF) - 2 ** 15
    half_ref[0:w, :] = jnp.where(hi > t_hi, 2 ** 15 - 1,
                                 jnp.where(hi < t_hi, -2 ** 15, lo)).astype(jnp.int16)
    t_lo = search16()
    thr = jnp.maximum(t_hi * 2 ** 16 + (t_lo + 2 ** 15), INT_MIN + 1)
    bias = jnp.where(key_ref[0:w, :] >= thr, 0.0, NEG_INF).T

    kk = k_ref[0:w, :].astype(MXU_DTYPE)
    lane = lax.broadcasted_iota(jnp.int32, (w, LANES), 1)
    v_aug = jnp.concatenate([v_ref[0:w, :].astype(MXU_DTYPE),
                             jnp.where(lane == 0, 1.0, 0.0).astype(MXU_DTYPE)], axis=1)
    for h in range(C_HEADS):
        q_r = qa_ref if h < C_HEADS // 2 else qb_ref
        hh = h % (C_HEADS // 2)
        s = _dot_nt(q_r[:, hh * C_HEAD_DIM:(hh + 1) * C_HEAD_DIM], kk) + bias
        p = jnp.exp(s - jnp.max(s, axis=-1, keepdims=True))
        pv = _dot(p, v_aug)
        o_ref[:, h * C_HEAD_DIM:(h + 1) * C_HEAD_DIM] = (
            pv[:, :C_HEAD_DIM] / pv[:, C_HEAD_DIM:C_HEAD_DIM + 1]).astype(o_ref.dtype)


def _dsa_kernel(qa_ref, qb_ref, iq_ref, iwq_ref, k_ref, v_ref, ikw_ref, o_ref, key_ref, half_ref,
                *, k_sel):
    n = pl.program_id(1)
    chunk = k_ref.shape[0] // DSA_WIDTHS
    n_chunks = (n * BLOCK + BLOCK - 1) // chunk + 1
    for wi in range(1, DSA_WIDTHS + 1):
        if (wi * chunk) % BLOCK:
            continue

        @pl.when(n_chunks == wi)
        def _(wi=wi):
            _dsa_block(qa_ref, qb_ref, iq_ref, iwq_ref, k_ref, v_ref, ikw_ref, o_ref, key_ref,
                       half_ref, n, wi * chunk, k_sel)


def _dsa(u, p1, b, s):
    nb = s // BLOCK
    k_sel = min(INDEX_TOPK, s // 4)
    half_q = C_HEADS * C_HEAD_DIM // 2
    k_blk = C_Q_RANK // LANES
    row = lambda bi, n: bi * nb + n
    return pl.pallas_call(
        functools.partial(_dsa_kernel, k_sel=k_sel), grid=(b, nb),
        in_specs=[pl.BlockSpec((BLOCK, half_q), lambda bi, n: (row(bi, n), 0)),
                  pl.BlockSpec((BLOCK, half_q), lambda bi, n: (row(bi, n), 1)),
                  pl.BlockSpec((BLOCK, half_q), lambda bi, n: (row(bi, n), 2)),
                  pl.BlockSpec((BLOCK, LANES), lambda bi, n: (row(bi, n), k_blk + 2)),
                  pl.BlockSpec((s, LANES), lambda bi, n: (bi, k_blk)),
                  pl.BlockSpec((s, LANES), lambda bi, n: (bi, k_blk + 1)),
                  pl.BlockSpec((s, LANES), lambda bi, n: (bi, k_blk + 2))],
        out_specs=pl.BlockSpec((BLOCK, C_HEADS * C_HEAD_DIM), lambda bi, n: (row(bi, n), 0)),
        out_shape=jax.ShapeDtypeStruct((b * s, C_HEADS * C_HEAD_DIM), MXU_DTYPE),
        scratch_shapes=[pltpu.VMEM((s, BLOCK), jnp.int32), pltpu.VMEM((s, BLOCK), jnp.int16)],
        compiler_params=_cparams("parallel", "arbitrary"), name="dsa",
    )(u, u, u, p1, p1, p1, p1)


ROUTER_TOKENS = 512
N_FULL_K0 = 8
SUB = 8


def _top_rows(work, k, exact):
    rows, cols = work.shape
    vrow = lax.broadcasted_iota(jnp.int32, (k, cols), 0)
    vals = jnp.zeros((k, cols), F32)
    if exact:
        row = lax.broadcasted_iota(jnp.int32, (rows, cols), 0).astype(F32)
        rank = jnp.full((rows, cols), float(k), F32)
    for r in range(k):
        m = jnp.max(work, axis=0, keepdims=True)
        hit = work == m
        if exact:
            first = jnp.min(jnp.where(hit, row, float(rows)), axis=0, keepdims=True)
            hit = row == first
            rank = jnp.where(hit, float(r), rank)
            work = jnp.where(hit, -jnp.inf, work)
        else:
            work = jnp.where(hit, -(2.0 ** 127) * (1.0 + r / 16.0), work)
        vals = jnp.where(vrow == r, m, vals)
    if not exact:
        assert k == 16
        rank = jnp.where(work <= -(2.0 ** 127), work * -(2.0 ** -123) - 16.0, float(k))
    return rank, vals


def _count_ranked(rank, k):
    return jnp.sum(jnp.where(rank < float(k), 1.0, 0.0), axis=0, keepdims=True)


def _pair_counts_exact(v0, v1, k):
    cand = jnp.concatenate([v0[k0:k0 + 1] + v1 for k0 in range(N_FULL_K0)]
                           + [v0[N_FULL_K0:] + v1[0:1]], axis=0)
    rank_c, _ = _top_rows(cand, k, True)
    sel = rank_c < float(k)
    cnt = jnp.where(sel, 1.0, 0.0)
    n_rank = jnp.concatenate(
        [jnp.sum(cnt[k0 * k:(k0 + 1) * k], axis=0, keepdims=True) for k0 in range(N_FULL_K0)]
        + [cnt[N_FULL_K0 * k:]], axis=0)
    z = jnp.sum(jnp.where(sel, jnp.exp(cand - cand[0:1]), 0.0), axis=0, keepdims=True)
    return n_rank, z, jnp.sum(cnt, axis=0, keepdims=True)


def _pair_counts_fast(v0, v1, k):
    cols = v0.shape[1]
    row = lax.broadcasted_iota(jnp.int32, (SUB, cols), 0)
    tiles = []
    for k1 in range(SUB):
        n_valid = min(k // (k1 + 1), SUB)
        tiles.append(jnp.where(row < n_valid, v0[0:SUB] + v1[k1:k1 + 1], -jnp.inf))
    tiles.append(v0[0:1] + v1[SUB:])
    tiles.append(v0[SUB:] + v1[0:1])
    cand = jnp.concatenate(tiles, axis=0)
    rank_c, _ = _top_rows(cand, k, False)
    sel = rank_c < float(k)
    cnt = jnp.where(sel, 1.0, 0.0)
    low = cnt[0:SUB]
    for k1 in range(1, SUB):
        low = low + cnt[k1 * SUB:(k1 + 1) * SUB]
    tail = jnp.sum(cnt[SUB * SUB:SUB * SUB + SUB], axis=0, keepdims=True)
    low = low + jnp.where(row == 0, tail, 0.0)
    n_rank = jnp.concatenate([low, cnt[SUB * SUB + SUB:]], axis=0)
    z = jnp.sum(jnp.where(sel, jnp.exp(cand - (v0[0:1] + v1[0:1])), 0.0), axis=0, keepdims=True)
    return n_rank, z, jnp.sum(cnt, axis=0, keepdims=True)


def _router_kernel(q_ref, keys_ref, p0_ref, n0_ref, p1_ref, r1_ref):
    k = PEER_TOPK

    def head(h, carry):
        col0 = pl.multiple_of(h * (2 * PEER_HALF), LANES)
        col1 = pl.multiple_of(col0 + PEER_HALF, LANES)
        s0 = _dot_nt(keys_ref[2 * h], q_ref[:, pl.ds(col0, PEER_HALF)])
        s1 = _dot_nt(keys_ref[2 * h + 1], q_ref[:, pl.ds(col1, PEER_HALF)])

        def route(exact):
            rank0, v0 = _top_rows(s0, k, exact)
            rank1, v1 = _top_rows(s1, k, exact)
            n_rank, z, n_sel = (_pair_counts_exact if exact else _pair_counts_fast)(v0, v1, k)
            n0 = jnp.zeros_like(s0)
            for r in range(k):
                n0 = jnp.where(rank0 == float(r), n_rank[r:r + 1], n0)
            p0_ref[h] = jnp.where(rank0 < float(k), jnp.exp(s0 - v0[0:1]), 0.0) / z
            n0_ref[h] = n0
            p1_ref[h] = jnp.where(rank1 < float(k), jnp.exp(s1 - v1[0:1]), 0.0).astype(p1_ref.dtype)
            r1_ref[h] = rank1.astype(r1_ref.dtype)
            return _count_ranked(rank0, k) + _count_ranked(rank1, k) + n_sel

        n_marked = route(False)

        @pl.when(jnp.max(n_marked) > 3.0 * k)
        def _():
            route(True)

        return carry

    lax.fori_loop(0, PEER_HEADS, head, 0)


def _router(q, keys):
    t = q.shape[0]
    tb = ROUTER_TOKENS
    blk = pl.BlockSpec((PEER_HEADS, PEER_N_KEYS, tb), lambda i: (0, 0, i))
    f32s = jax.ShapeDtypeStruct((PEER_HEADS, PEER_N_KEYS, t), F32)
    bf16s = jax.ShapeDtypeStruct((PEER_HEADS, PEER_N_KEYS, t), BF16)
    return pl.pallas_call(
        _router_kernel, grid=(t // tb,),
        in_specs=[pl.BlockSpec((tb, q.shape[1]), lambda i: (i, 0)),
                  pl.BlockSpec(keys.shape, lambda i: (0, 0, 0))],
        out_specs=[blk, blk, blk, blk], out_shape=[f32s, f32s, bf16s, bf16s],
        compiler_params=_cparams("parallel"), name="peer_router",
    )(q, keys)


EXPERT_TOKENS = 512
EXPERT_CHUNK = 1024
ROWS_PER_CHUNK = EXPERT_CHUNK // PEER_N_KEYS


MM1_SLICES = 2
PACK = 16


def _gate_tile(pre_ref, p0_ref, n0_ref, p1_ref, r1_ref, h_ref, tile, cache):
    tb = pre_ref.shape[1]
    ii, rt = divmod(tile, PEER_N_KEYS // PACK)
    if ii not in cache:
        cache.clear()
        cache[ii] = (
            [jnp.broadcast_to(p0_ref[h, ii:ii + 1, :], (PACK, tb)).astype(BF16)
             for h in range(PEER_HEADS)],
            [jnp.broadcast_to(n0_ref[h, ii:ii + 1, :], (PACK, tb)).astype(BF16)
             for h in range(PEER_HEADS)])
    gates, cnts = cache[ii]
    js = slice(rt * PACK, (rt + 1) * PACK)
    w = None
    for h in range(PEER_HEADS):
        term = jnp.where(r1_ref[h, js, :] < cnts[h], p1_ref[h, js, :],
                         jnp.zeros((), BF16)) * gates[h]
        w = term if w is None else w + term
    rows = slice(tile * PACK, (tile + 1) * PACK)
    pre = pre_ref[rows, :]
    act = 0.5 * pre * (1.0 + lax.erf(pre * np.sqrt(0.5).astype(np.float32)))
    h_ref[rows, :] = act.astype(BF16) * w


def _expert_kernel(*refs, with_norm):
    xn_ref, x_ref, u_ref, vt_ref, p0_ref, n0_ref, p1_ref, r1_ref = refs[:8]
    g_ref = refs[8] if with_norm else None
    o_ref, acc_ref, pre_ref, h_ref = refs[-4:]
    c = pl.program_id(1)

    @pl.when(c == 0)
    def _():
        acc_ref[...] = jnp.zeros_like(acc_ref)

    cache = {}
    n_tiles = EXPERT_CHUNK // PACK
    for k in range(MM1_SLICES):
        rows = slice(k * EXPERT_CHUNK // MM1_SLICES, (k + 1) * EXPERT_CHUNK // MM1_SLICES)
        pre_ref[rows, :] = _dot_nt(u_ref[rows, :], xn_ref[...])
        for tile in range(k * n_tiles // MM1_SLICES, (k + 1) * n_tiles // MM1_SLICES):
            _gate_tile(pre_ref, p0_ref, n0_ref, p1_ref, r1_ref, h_ref, tile, cache)
    acc_ref[...] += jnp.dot(vt_ref[...], h_ref[...], preferred_element_type=F32)

    @pl.when(c == pl.num_programs(1) - 1)
    def _():
        y = x_ref[...] + acc_ref[...].T
        if with_norm:
            ms = jnp.mean(y * y, axis=-1, keepdims=True)
            y = y * lax.rsqrt(ms + EPS) * g_ref[...]
        o_ref[...] = y


def _experts(xn, x, u, vt, p0, n0, p1, r1, out_gain=None):
    t, d = x.shape
    tb, ec = EXPERT_TOKENS, EXPERT_CHUNK
    small = pl.BlockSpec((PEER_HEADS, ROWS_PER_CHUNK, tb), lambda i, c: (0, c, i))
    full = pl.BlockSpec((PEER_HEADS, PEER_N_KEYS, tb), lambda i, c: (0, 0, i))
    with_norm = out_gain is not None
    gain = ([pl.BlockSpec((1, d), lambda i, c: (0, 0))], [out_gain.reshape(1, d)]) if with_norm else ([], [])
    return pl.pallas_call(
        functools.partial(_expert_kernel, with_norm=with_norm), grid=(t // tb, PEER_EXPERTS // ec),
        in_specs=[pl.BlockSpec((tb, d), lambda i, c: (i, 0)),
                  pl.BlockSpec((tb, d), lambda i, c: (i, 0)),
                  pl.BlockSpec((ec, d), lambda i, c: (c, 0)),
                  pl.BlockSpec((None, d, ec), lambda i, c: (c, 0, 0)),
                  small, small, full, full] + gain[0],
        out_specs=pl.BlockSpec((tb, d), lambda i, c: (i, 0)),
        out_shape=jax.ShapeDtypeStruct((t, d), F32),
        scratch_shapes=[pltpu.VMEM((d, tb), F32), pltpu.VMEM((ec, tb), F32),
                        pltpu.VMEM((ec, tb), BF16)],
        compiler_params=_cparams("parallel", "arbitrary"), name="peer_experts",
    )(xn, x, u, vt, p0, n0, p1, r1, *gain[1])


def _peer(x, g, wq, keys, u, v, out_gain=None):
    xq, xn = _proj(x, g, wq.astype(MXU_DTYPE), k_cols=x.shape[1], tm=512, tn=wq.shape[1],
                   with_xn=True, out_dtype=MXU_DTYPE)
    keys2 = keys.reshape(2 * PEER_HEADS, PEER_N_KEYS, PEER_HALF).astype(MXU_DTYPE)
    p0, n0, p1, r1 = _router(xq, keys2)
    nc = PEER_EXPERTS // EXPERT_CHUNK
    vt = v.astype(MXU_DTYPE).reshape(nc, EXPERT_CHUNK, -1).transpose(0, 2, 1)
    return _experts(xn, x, u.astype(MXU_DTYPE), vt, p0, n0, p1, r1, out_gain)


def kernel(x, positions, norm_mix0, w_in0, sinks0, conv_w0, w_out0, norm_ffn0, peer_wq0, peer_keys0,
           peer_u0, peer_v0, norm_mix1, w_in1, g_qa1, w_uq1, w_out1, norm_ffn1, peer_wq1,
           peer_keys1, peer_u1, peer_v1, norm_final):
    b, s, d = x.shape
    t = b * s
    xf = x.reshape(t, d)
    pos = positions.reshape(t)
    tab_a = _rope_table(pos, ((0, 64), (64, 64)), 8)
    tab_b = _rope_table(pos, ((0, 128),), 16)
    tab_c = _rope_table(pos, ((0, 64),), 8)

    p0, b_out = _in0(xf, norm_mix0, w_in0.astype(MXU_DTYPE), tab_a, conv_w0, s)
    a_out = _swa(p0, sinks0, b, s)
    w_o = w_out0.astype(MXU_DTYPE)
    x1 = _mm_res([a_out, b_out], [w_o[:A_Q_COLS], w_o[A_Q_COLS:]], xf, tm=512, tn=d)
    x2 = _peer(x1, norm_ffn0, peer_wq0, peer_keys0, peer_u0, peer_v0)

    w1 = jnp.pad(w_in1, ((0, 0), (0, IN1_COLS_PADDED - w_in1.shape[1]))).astype(MXU_DTYPE)
    k_chunk = C_Q_RANK // LANES
    cfg1 = tuple({k_chunk: 0, k_chunk + 2: 1}.get(c, -1) for c in range(IN1_COLS_PADDED // LANES))
    p1 = _proj(x2, norm_mix1, w1, k_cols=d, tm=512, tn=IN1_COLS_PADDED,
               tabs=(tab_b, tab_c), halves=(16, 8), groups=(((0, 1), cfg1),))
    n_q = C_HEADS * C_HEAD_DIM // LANES
    n_iq = IDX_HEADS * IDX_HEAD_DIM // LANES
    cfg_u = (0,) * n_q + (1,) * n_iq
    u = _proj(p1, g_qa1, w_uq1.astype(MXU_DTYPE), k_cols=C_Q_RANK, tm=256,
              tn=(n_q + n_iq) * LANES, tabs=(tab_b, tab_a), halves=(16, 8),
              groups=(((0, 1), cfg_u),), scales=(C_HEAD_DIM ** -0.5,) * n_q + (1.0,) * n_iq,
              out_dtype=MXU_DTYPE)
    o = _dsa(u, p1, b, s)
    x3 = _mm_res([o], [w_out1.astype(MXU_DTYPE)], x2, tm=512, tn=d)
    out = _peer(x3, norm_ffn1, peer_wq1, peer_keys1, peer_u1, peer_v1, out_gain=norm_final)
    return out.reshape(b, s, d)
```
